```python
import math
import jax, jax.numpy as jnp
from jax import lax
import numpy as np

D_MODEL = 2048
BATCH = 8
SEQ = 2048
DEPTH = 2
DEC_BATCH = 32
DEC_SEQ = 1
PAST_LEN = 8192
PAGE_SIZE = 128

SSM_HEAD_DIM = 64
SSM_INNER = D_MODEL
SSM_HEADS = SSM_INNER // SSM_HEAD_DIM
SSM_GROUPS = 4
SSM_HEADS_PER_GROUP = SSM_HEADS // SSM_GROUPS
SSM_STATE = 128
CONV_WIDTH = 4
CONV_CH = SSM_INNER + 2 * SSM_GROUPS * SSM_STATE
SCAN_CHUNK = 128
ATT_HEAD_DIM = 128
ATT_HEADS = D_MODEL // ATT_HEAD_DIM
ATT_KV_HEADS = 4
ATT_Q_PER_KV = ATT_HEADS // ATT_KV_HEADS
IDX_HEADS = 8
IDX_DIM = 64
TOPK_MAX = 256
QUERY_BLOCK = 64
ROPE_THETA = 500000.0
ROPE_FRACTION = 4
RET_HEADS = 8
RET_VALUE_DIM = D_MODEL // RET_HEADS
RET_KEY_DIM = RET_VALUE_DIM // 2
RET_THETA = 10000.0
D_FF = 5632
MACARON_WEIGHT = 0.5
N_BRANCH = 3
N_MOD = 9
NORM_EPS = 1e-6

IN_WIDTHS = (SSM_INNER, CONV_CH, SSM_HEADS,
             ATT_HEADS * ATT_HEAD_DIM, ATT_KV_HEADS * ATT_HEAD_DIM, ATT_KV_HEADS * ATT_HEAD_DIM,
             IDX_HEADS * IDX_DIM, IDX_DIM, IDX_HEADS,
             RET_HEADS * RET_KEY_DIM, RET_HEADS * RET_KEY_DIM, RET_HEADS * RET_VALUE_DIM, RET_HEADS * RET_VALUE_DIM,
             N_BRANCH * D_MODEL)
IN_TOTAL = sum(IN_WIDTHS)

kernel_name = 'hybrid_ssd_dsa_retention_macaron_step'


def rms_norm(x, g):
    xf = x.astype(jnp.float32)
    y = xf * lax.rsqrt(jnp.mean(xf * xf, axis=-1, keepdims=True) + NORM_EPS)
    return (y * g.astype(jnp.float32)).astype(x.dtype)


def rotary(x, pos, rot_dim, theta):
    half = rot_dim // 2
    inv_freq = theta ** (-jnp.arange(half, dtype=jnp.float32) * (2.0 / rot_dim))
    ang = pos.astype(jnp.float32)[:, None] * inv_freq[None, :]
    ang = ang.reshape((ang.shape[0],) + (1,) * (x.ndim - 3) + (half,))
    cos, sin = jnp.cos(ang), jnp.sin(ang)
    xr = x[..., :rot_dim].astype(jnp.float32)
    x1, x2 = xr[..., :half], xr[..., half:]
    rotated = jnp.concatenate([x1 * cos - x2 * sin, x2 * cos + x1 * sin], axis=-1).astype(x.dtype)
    return jnp.concatenate([rotated, x[..., rot_dim:]], axis=-1)


def swiglu(h, w_gate, w_up, w_down):
    return (jax.nn.silu(h @ w_gate) * (h @ w_up)) @ w_down


def gather_rows(a, idx):
    return jax.vmap(lambda ab, ib: ab[ib])(a, idx)


def chunked_scan(q, k, v, log_a, s0):
    B, L, G, N = q.shape
    R, P = v.shape[3], v.shape[4]
    Q = SCAN_CHUNK if L % SCAN_CHUNK == 0 else L
    nc = L // Q
    qc = q.reshape(B, nc, Q, G, N)
    kc = k.reshape(B, nc, Q, G, N)
    vc = v.reshape(B, nc, Q, G, R, P)
    acs = jnp.cumsum(log_a.astype(jnp.float32).reshape(B, nc, Q, G, R), axis=2)
    causal = jnp.tril(jnp.ones((Q, Q), dtype=bool))[:, :, None, None]
    seg = acs[:, :, :, None] - acs[:, :, None, :]
    decay = jnp.exp(jnp.where(causal, seg, -jnp.inf))
    qk = jnp.einsum('bcign,bcjgn->bcijg', qc, kc)
    y_diag = jnp.einsum('bcijgr,bcjgrp->bcigrp', qk[..., None] * decay, vc)
    to_end = jnp.exp(acs[:, :, -1:] - acs)
    chunk_states = jnp.einsum('bcjgn,bcjgrp->bcgrpn', kc, vc * to_end[..., None])
    chunk_decay = jnp.exp(acs[:, :, -1])

    def step(s, inp):
        cs, cd = inp
        return s * cd[..., None, None] + cs, s

    s_fin, s_in = lax.scan(step, s0.astype(jnp.float32),
                           (jnp.moveaxis(chunk_states, 1, 0), jnp.moveaxis(chunk_decay, 1, 0)))
    s_in = jnp.moveaxis(s_in, 0, 1)
    y_off = jnp.einsum('bcign,bcgrpn->bcigrp', qc, s_in) * jnp.exp(acs)[..., None]
    y = (y_diag + y_off).reshape(B, L, G, R, P).astype(v.dtype)
    return y, s_fin


def ssd_branch(z, xbc, dt_raw, conv_buf, s0, conv_w, conv_b, a_log, dt_bias, d_skip, norm_g):
    B, L, _ = z.shape
    xp = jnp.concatenate([conv_buf, xbc], axis=1)
    conv = conv_b + sum(xp[:, i:i + L] * conv_w[i] for i in range(CONV_WIDTH))
    new_buf = xp[:, L:]
    xbc = jax.nn.silu(conv)
    xs, b_in, c_out = jnp.split(xbc, [SSM_INNER, SSM_INNER + SSM_GROUPS * SSM_STATE], axis=-1)
    xs = xs.reshape(B, L, SSM_GROUPS, SSM_HEADS_PER_GROUP, SSM_HEAD_DIM)
    b_in = b_in.reshape(B, L, SSM_GROUPS, SSM_STATE)
    c_out = c_out.reshape(B, L, SSM_GROUPS, SSM_STATE)
    dt = jax.nn.softplus(dt_raw.astype(jnp.float32) + dt_bias.astype(jnp.float32))
    dt = dt.reshape(B, L, SSM_GROUPS, SSM_HEADS_PER_GROUP)
    a = -jnp.exp(a_log.astype(jnp.float32)).reshape(SSM_GROUPS, SSM_HEADS_PER_GROUP)
    y, s_fin = chunked_scan(c_out, b_in, xs * dt[..., None], dt * a,
                            s0.reshape(B, SSM_GROUPS, SSM_HEADS_PER_GROUP, SSM_HEAD_DIM, SSM_STATE))
    y = y + xs * d_skip.reshape(SSM_GROUPS, SSM_HEADS_PER_GROUP, 1)
    y = y.reshape(B, L, SSM_INNER) * jax.nn.silu(z)
    yg = y.astype(jnp.float32).reshape(B, L, SSM_GROUPS, SSM_INNER // SSM_GROUPS)
    yg = yg * lax.rsqrt(jnp.mean(yg * yg, axis=-1, keepdims=True) + NORM_EPS)
    y = (yg.reshape(B, L, SSM_INNER) * norm_g.astype(jnp.float32)).astype(z.dtype)
    return y, new_buf, s_fin.reshape(B, SSM_HEADS, SSM_HEAD_DIM, SSM_STATE).astype(z.dtype)


def retention_branch(q, k, v, g, pos, s0):
    B, L = q.shape[:2]
    q = rotary(q, pos, RET_KEY_DIM, RET_THETA)
    k = rotary(k, pos, RET_KEY_DIM, RET_THETA) * (RET_KEY_DIM ** -0.5)
    log_gamma = jnp.log(1.0 - 2.0 ** (-5.0 - jnp.arange(RET_HEADS, dtype=jnp.float32)))
    log_a = jnp.broadcast_to(log_gamma[:, None], (B, L, RET_HEADS, 1))
    y, s_fin = chunked_scan(q, k, v[:, :, :, None, :], log_a, s0[:, :, None])
    yf = y[:, :, :, 0].astype(jnp.float32)
    yf = yf * lax.rsqrt(jnp.mean(yf * yf, axis=-1, keepdims=True) + NORM_EPS)
    out = jax.nn.silu(g) * yf.reshape(B, L, RET_HEADS * RET_VALUE_DIM).astype(g.dtype)
    return out, s_fin[:, :, 0].astype(v.dtype)


def dsa_select(qi, wi, ki, q_pos, k_top):
    s = jnp.einsum('bthd,bsd->bths', qi, ki) * (IDX_DIM ** -0.5)
    score = jnp.einsum('bths,bth->bts', jax.nn.relu(s), wi).astype(jnp.float32)
    key_pos = jnp.arange(ki.shape[1])
    admissible = key_pos[None, :] <= q_pos[:, None]
    score = jnp.where(admissible[None], score, -jnp.inf)
    _, idx = lax.top_k(score, k_top)
    valid = idx <= q_pos[None, :, None]
    return idx, valid


def attend_selected(q, kg, vg, valid):
    B, T = q.shape[:2]
    qg = q.reshape(B, T, ATT_KV_HEADS, ATT_Q_PER_KV, ATT_HEAD_DIM)
    logits = jnp.einsum('btkgd,btjkd->btkgj', qg, kg).astype(jnp.float32) * (ATT_HEAD_DIM ** -0.5)
    logits = jnp.where(valid[:, :, None, None, :], logits, -jnp.inf)
    p = jax.nn.softmax(logits, axis=-1)
    o = jnp.einsum('btkgj,btjkd->btkgd', p.astype(vg.dtype), vg)
    return o.reshape(B, T, ATT_HEADS * ATT_HEAD_DIM)


def dsa_prompt(q, k, v, qi, ki, wi):
    B, S = q.shape[:2]
    k_top = min(TOPK_MAX, S // 4)
    n_blocks = S // QUERY_BLOCK

    def block(bi):
        start = bi * QUERY_BLOCK
        qb = lax.dynamic_slice_in_dim(q, start, QUERY_BLOCK, axis=1)
        qib = lax.dynamic_slice_in_dim(qi, start, QUERY_BLOCK, axis=1)
        wib = lax.dynamic_slice_in_dim(wi, start, QUERY_BLOCK, axis=1)
        pos = start + jnp.arange(QUERY_BLOCK)
        idx, valid = dsa_select(qib, wib, ki, pos, k_top)
        return attend_selected(qb, gather_rows(k, idx), gather_rows(v, idx), valid)

    out = lax.map(block, jnp.arange(n_blocks))
    return jnp.swapaxes(out, 0, 1).reshape(B, S, ATT_HEADS * ATT_HEAD_DIM)


def dsa_sample(q, k, v, qi, ki, wi, cache_k, cache_v, cache_idx_k, page_table, layer):
    B, T = q.shape[:2]
    page = cache_k.shape[2]
    past_len = page_table.shape[1] * page
    ki_past = cache_idx_k[page_table, layer].reshape(B, past_len, IDX_DIM)
    ki_all = jnp.concatenate([ki_past, ki], axis=1)
    k_top = min(TOPK_MAX, (past_len + T) // 4)
    q_pos = past_len + jnp.arange(T)
    idx, valid = dsa_select(qi, wi, ki_all, q_pos, k_top)
    in_past = (idx < past_len)[..., None, None]
    p_idx = jnp.minimum(idx, past_len - 1)
    phys = jax.vmap(lambda pt, i: pt[i])(page_table, p_idx // page)
    off = p_idx % page
    n_idx = jnp.clip(idx - past_len, 0, T - 1)
    kg = jnp.where(in_past, cache_k[phys, layer, off], gather_rows(k, n_idx))
    vg = jnp.where(in_past, cache_v[phys, layer, off], gather_rows(v, n_idx))
    return attend_selected(q, kg, vg, valid)


def token_mixing(h, pos, l, conv_buf, ssm_s0, ret_s0, dsa_fn, W):
    B, L, _ = h.shape
    pts, acc = [], 0
    for w in IN_WIDTHS[:-1]:
        acc += w
        pts.append(acc)
    (z, xbc, dt_raw, aq, ak, av, iq, ik, iw, rq, rk, rv, rg, gate_logits) = jnp.split(h @ W['w_in'][l], pts, axis=-1)
    y_ssd, conv_new, ssm_new = ssd_branch(z, xbc, dt_raw, conv_buf, ssm_s0, W['conv_w'][l], W['conv_b'][l],
                                          W['ssm_a_log'][l], W['ssm_dt_bias'][l], W['ssm_d'][l], W['ssm_norm'][l])
    rot = ATT_HEAD_DIM // ROPE_FRACTION
    irot = IDX_DIM // ROPE_FRACTION
    aq = rotary(aq.reshape(B, L, ATT_HEADS, ATT_HEAD_DIM), pos, rot, ROPE_THETA)
    ak = rotary(ak.reshape(B, L, ATT_KV_HEADS, ATT_HEAD_DIM), pos, rot, ROPE_THETA)
    av = av.reshape(B, L, ATT_KV_HEADS, ATT_HEAD_DIM)
    iq = rotary(iq.reshape(B, L, IDX_HEADS, IDX_DIM), pos, irot, ROPE_THETA)
    ik = rotary(ik, pos, irot, ROPE_THETA)
    iw = iw * (IDX_HEADS ** -0.5)
    y_att = dsa_fn(l, aq, ak, av, iq, ik, iw)
    y_ret, ret_new = retention_branch(rq.reshape(B, L, RET_HEADS, RET_KEY_DIM), rk.reshape(B, L, RET_HEADS, RET_KEY_DIM),
                                      rv.reshape(B, L, RET_HEADS, RET_VALUE_DIM), rg, pos, ret_s0)
    branches = jnp.stack([y_ssd, y_att, y_ret], axis=2)
    proj = jnp.einsum('blmi,mid->blmd', branches, W['w_branch'][l])
    gates = jax.nn.sigmoid(gate_logits.reshape(B, L, N_BRANCH, D_MODEL))
    merged = jnp.sum(gates * proj, axis=2)
    return merged @ W['w_out'][l], ak, av, ik, conv_new, ssm_new, ret_new


def run_trunk(x, c, pos, conv_state, ssm_state, ret_state, dsa_fn, W):
    ks, vs, iks, convs, ssms, rets = [], [], [], [], [], []
    for l in range(DEPTH):
        mods = jnp.split(jax.nn.silu(c) @ W['ada_w'][l] + W['ada_b'][l], N_MOD, axis=-1)
        sh1, sc1, g1, sh2, sc2, g2, sh3, sc3, g3 = [m[:, None, :] for m in mods]
        h = rms_norm(x, W['norm_g'][l, 0]) * (1.0 + sc1) + sh1
        x = x + MACARON_WEIGHT * g1 * swiglu(h, W['ffn1_gate'][l], W['ffn1_up'][l], W['ffn1_down'][l])
        h = rms_norm(x, W['norm_g'][l, 1]) * (1.0 + sc2) + sh2
        mix, k_rows, v_rows, ik_rows, conv_new, ssm_new, ret_new = token_mixing(
            h, pos, l, conv_state[:, l], ssm_state[:, l], ret_state[:, l], dsa_fn, W)
        x = x + g2 * mix
        h = rms_norm(x, W['norm_g'][l, 2]) * (1.0 + sc3) + sh3
        x = x + MACARON_WEIGHT * g3 * swiglu(h, W['ffn2_gate'][l], W['ffn2_up'][l], W['ffn2_down'][l])
        ks.append(k_rows); vs.append(v_rows); iks.append(ik_rows)
        convs.append(conv_new); ssms.append(ssm_new); rets.append(ret_new)
    y = rms_norm(x, W['final_g'])
    st = lambda a: jnp.stack(a, axis=1)
    return y, st(ks), st(vs), st(iks), st(convs), st(ssms), st(rets)


def setup_inputs(seed: int = 0) -> dict:
    key = jax.random.key(seed)
    ks = jax.random.split(key, 32)

    def nrm(i, shape, scale):
        return jax.random.normal(ks[i], shape, jnp.float32) * scale

    n_pages = PAST_LEN // PAGE_SIZE
    n_pool = (DEC_BATCH * n_pages * 5) // 4
    page_table = jax.random.permutation(ks[8], n_pool)[: DEC_BATCH * n_pages].reshape(DEC_BATCH, n_pages).astype(jnp.int32)
    dt0 = jnp.exp(jax.random.uniform(ks[15], (DEPTH, SSM_HEADS), jnp.float32, math.log(1e-3), math.log(1e-1)))
    ssm_dt_bias = dt0 + jnp.log(-jnp.expm1(-dt0))
    ssm_a_log = jnp.log(jax.random.uniform(ks[14], (DEPTH, SSM_HEADS), jnp.float32, 1.0, 16.0))
    dm = D_MODEL ** -0.5
    return {
        'x_prompt': nrm(0, (BATCH, SEQ, D_MODEL), 1.0),
        'x_sample': nrm(1, (DEC_BATCH, DEC_SEQ, D_MODEL), 1.0),
        'cache_k': nrm(2, (n_pool, DEPTH, PAGE_SIZE, ATT_KV_HEADS, ATT_HEAD_DIM), 1.0),
        'cache_v': nrm(3, (n_pool, DEPTH, PAGE_SIZE, ATT_KV_HEADS, ATT_HEAD_DIM), 1.0),
        'cache_idx_k': nrm(4, (n_pool, DEPTH, PAGE_SIZE, IDX_DIM), 1.0),
        'state_conv': nrm(5, (DEC_BATCH, DEPTH, CONV_WIDTH - 1, CONV_CH), 1.0),
        'state_ssm': nrm(6, (DEC_BATCH, DEPTH, SSM_HEADS, SSM_HEAD_DIM, SSM_STATE), 0.1),
        'state_ret': nrm(7, (DEC_BATCH, DEPTH, RET_HEADS, RET_VALUE_DIM, RET_KEY_DIM), 0.1),
        'page_table': page_table,
        'c_prompt': nrm(9, (BATCH, D_MODEL), 1.0),
        'c_sample': nrm(10, (DEC_BATCH, D_MODEL), 1.0),
        'w_in': nrm(11, (DEPTH, D_MODEL, IN_TOTAL), dm),
        'conv_w': nrm(12, (DEPTH, CONV_WIDTH, CONV_CH), CONV_WIDTH ** -0.5),
        'conv_b': nrm(13, (DEPTH, CONV_CH), 0.01),
        'ssm_a_log': ssm_a_log,
        'ssm_dt_bias': ssm_dt_bias,
        'ssm_d': 1.0 + nrm(16, (DEPTH, SSM_HEADS), 0.01),
        'ssm_norm': 1.0 + nrm(17, (DEPTH, SSM_INNER), 0.01),
        'w_branch': nrm(18, (DEPTH, N_BRANCH, D_MODEL, D_MODEL), dm),
        'w_out': nrm(19, (DEPTH, D_MODEL, D_MODEL), dm),
        'ffn1_gate': nrm(20, (DEPTH, D_MODEL, D_FF), dm),
        'ffn1_up': nrm(21, (DEPTH, D_MODEL, D_FF), dm),
        'ffn1_down': nrm(22, (DEPTH, D_FF, D_MODEL), D_FF ** -0.5),
        'ffn2_gate': nrm(23, (DEPTH, D_MODEL, D_FF), dm),
        'ffn2_up': nrm(24, (DEPTH, D_MODEL, D_FF), dm),
        'ffn2_down': nrm(25, (DEPTH, D_FF, D_MODEL), D_FF ** -0.5),
        'norm_g': 1.0 + nrm(26, (DEPTH, 3, D_MODEL), 0.01),
        'final_g': 1.0 + nrm(27, (D_MODEL,), 0.01),
        'ada_w': nrm(28, (DEPTH, D_MODEL, N_MOD * D_MODEL), 0.5 * dm),
        'ada_b': nrm(29, (DEPTH, N_MOD * D_MODEL), 0.01),
    }


def reference(x_prompt, x_sample, cache_k, cache_v, cache_idx_k, state_conv, state_ssm, state_ret, page_table,
              c_prompt, c_sample, w_in, conv_w, conv_b, ssm_a_log, ssm_dt_bias, ssm_d, ssm_norm, w_branch, w_out,
              ffn1_gate, ffn1_up, ffn1_down, ffn2_gate, ffn2_up, ffn2_down, norm_g, final_g, ada_w, ada_b):
    W = {'w_in': w_in, 'conv_w': conv_w, 'conv_b': conv_b, 'ssm_a_log': ssm_a_log, 'ssm_dt_bias': ssm_dt_bias,
         'ssm_d': ssm_d, 'ssm_norm': ssm_norm, 'w_branch': w_branch, 'w_out': w_out,
         'ffn1_gate': ffn1_gate, 'ffn1_up': ffn1_up, 'ffn1_down': ffn1_down,
         'ffn2_gate': ffn2_gate, 'ffn2_up': ffn2_up, 'ffn2_down': ffn2_down,
         'norm_g': norm_g, 'final_g': final_g, 'ada_w': ada_w, 'ada_b': ada_b}
    bp, seq = x_prompt.shape[0], x_prompt.shape[1]
    pos_p = jnp.arange(seq)
    conv0 = jnp.zeros((bp, DEPTH, CONV_WIDTH - 1, CONV_CH), x_prompt.dtype)
    ssm0 = jnp.zeros((bp, DEPTH, SSM_HEADS, SSM_HEAD_DIM, SSM_STATE), x_prompt.dtype)
    ret0 = jnp.zeros((bp, DEPTH, RET_HEADS, RET_VALUE_DIM, RET_KEY_DIM), x_prompt.dtype)
    prompt_dsa = lambda l, q, k, v, qi, ki, wi: dsa_prompt(q, k, v, qi, ki, wi)
    y_prompt, k_p, v_p, ik_p, conv_p, ssm_p, ret_p = run_trunk(x_prompt, c_prompt, pos_p, conv0, ssm0, ret0, prompt_dsa, W)
    past_len = page_table.shape[1] * cache_k.shape[2]
    pos_s = past_len + jnp.arange(x_sample.shape[1])
    sample_dsa = lambda l, q, k, v, qi, ki, wi: dsa_sample(q, k, v, qi, ki, wi, cache_k, cache_v, cache_idx_k, page_table, l)
    y_sample, k_s, v_s, ik_s, conv_s, ssm_s, ret_s = run_trunk(x_sample, c_sample, pos_s, state_conv, state_ssm, state_ret, sample_dsa, W)
    return (y_prompt, y_sample, k_p, v_p, ik_p, conv_p, ssm_p, ret_p, k_s, v_s, ik_s, conv_s, ssm_s, ret_s)
```

```python
import functools

import jax
import jax.numpy as jnp
from jax import lax
from jax.experimental import pallas as pl
from jax.experimental.pallas import tpu as pltpu

DEPTH = 2
SSM_HEAD_DIM = 64
SSM_GROUPS = 4
SSM_STATE = 128
CONV_WIDTH = 4
SCAN_CHUNK = 128
ATT_HEAD_DIM = 128
ATT_KV_HEADS = 4
IDX_HEADS = 8
IDX_DIM = 64
TOPK_MAX = 256
ROPE_THETA = 500000.0
ROPE_FRACTION = 4
RET_HEADS = 8
RET_THETA = 10000.0
MACARON_WEIGHT = 0.5
N_BRANCH = 3
N_MOD = 9
NORM_EPS = 1e-6

LANES = 128
VMEM_LIMIT_BYTES = 56 * 1024 * 1024

F32 = jnp.float32
BF16 = jnp.bfloat16


def _params(*sem):
    return pltpu.CompilerParams(dimension_semantics=sem, vmem_limit_bytes=VMEM_LIMIT_BYTES)


def _dot(a, b):
    return jnp.dot(a, b, preferred_element_type=F32)


def _dot_nt(a, b):
    return lax.dot_general(a, b, (((1,), (1,)), ((), ())), preferred_element_type=F32)


def _dot_tn(a, b):
    return lax.dot_general(a, b, (((0,), (0,)), ((), ())), preferred_element_type=F32)


def _split3(x):
    x1 = x.astype(BF16)
    r1 = x - x1.astype(F32)
    x2 = r1.astype(BF16)
    x3 = (r1 - x2.astype(F32)).astype(BF16)
    return x1, x2, x3


def _split2(x):
    x1 = x.astype(BF16)
    return x1, (x - x1.astype(F32)).astype(BF16)


def _sel_right(x, onehot):
    x1, x2, x3 = _split3(x)
    return _dot(x1, onehot) + _dot(x2, onehot) + _dot(x3, onehot)


def _sel_left(onehot, x):
    x1, x2, x3 = _split3(x)
    return _dot(onehot, x1) + _dot(onehot, x2) + _dot(onehot, x3)


def _silu(x):
    return x * jax.nn.sigmoid(x)


def _softplus(x):
    return jnp.maximum(x, 0.0) + jnp.log1p(jnp.exp(-jnp.abs(x)))


def _iota(shape, dim):
    return lax.broadcasted_iota(jnp.int32, shape, dim)


def _ada_kernel(c_ref, w_ref, b_ref, o_ref):
    a = _silu(c_ref[...]).astype(BF16)
    o_ref[...] = _dot(a, w_ref[...].astype(BF16)) + b_ref[...]


def _ada(c_all, ada_w, ada_b):
    depth, d, n = ada_w.shape
    rows = c_all.shape[0]
    tn = 1024
    return pl.pallas_call(
        _ada_kernel,
        grid=(depth, n // tn),
        in_specs=[pl.BlockSpec((rows, d), lambda l, j: (0, 0)),
                  pl.BlockSpec((None, d, tn), lambda l, j: (l, 0, j)),
                  pl.BlockSpec((None, 1, tn), lambda l, j: (l, 0, j))],
        out_specs=pl.BlockSpec((None, rows, tn), lambda l, j: (l, 0, j)),
        out_shape=jax.ShapeDtypeStruct((depth, rows, n), F32),
        compiler_params=_params("parallel", "parallel"),
        name="ada_mod",
    )(c_all, ada_w, ada_b.reshape(depth, 1, n))


class _Group:
    def __init__(self, batch, seq, mods, tm):
        self.batch, self.seq, self.m, self.tm = batch, seq, batch * seq, tm
        self.per_row = seq == 1
        self.mods = mods

    def mod(self, layer, which, tn, two_d):
        arr = self.mods[layer]
        if self.per_row:
            shape = (None, self.batch, tn)
            if two_d:
                return arr, pl.BlockSpec(shape, lambda i, j: (which, 0, j))
            return arr, pl.BlockSpec(shape, lambda i: (which, 0, 0))
        tm, seq = self.tm, self.seq
        shape = (None, 1, tn)
        if two_d:
            return arr, pl.BlockSpec(shape, lambda i, j: ((i * tm // seq) * N_MOD + which, 0, j))
        return arr, pl.BlockSpec(shape, lambda i: ((i * tm // seq) * N_MOD + which, 0, 0))


def _normmod_kernel(x_ref, g_ref, sc_ref, sh_ref, o_ref):
    x = x_ref[...]
    y = x * lax.rsqrt(jnp.mean(x * x, axis=-1, keepdims=True) + NORM_EPS) * g_ref[...]
    o_ref[...] = (y * (1.0 + sc_ref[...]) + sh_ref[...]).astype(o_ref.dtype)


def _normmod(grp, x, g, layer, w_shift, w_scale):
    d = x.shape[1]
    tm = min(256, grp.tm)
    sub = _Group(grp.batch, grp.seq, grp.mods, tm)
    sc_arr, sc_spec = sub.mod(layer, w_scale, d, False)
    sh_arr, sh_spec = sub.mod(layer, w_shift, d, False)
    return pl.pallas_call(
        _normmod_kernel,
        grid=(grp.m // tm,),
        in_specs=[pl.BlockSpec((tm, d), lambda i: (i, 0)),
                  pl.BlockSpec((1, d), lambda i: (0, 0)), sc_spec, sh_spec],
        out_specs=pl.BlockSpec((tm, d), lambda i: (i, 0)),
        out_shape=jax.ShapeDtypeStruct((grp.m, d), BF16),
        compiler_params=_params("parallel"),
        name="norm_mod",
    )(x, g.reshape(1, d), sc_arr, sh_arr)


def _final_norm_kernel(x_ref, g_ref, o_ref):
    x = x_ref[...]
    o_ref[...] = x * lax.rsqrt(jnp.mean(x * x, axis=-1, keepdims=True) + NORM_EPS) * g_ref[...]


def _final_norm(grp, x, g):
    d = x.shape[1]
    tm = min(256, grp.tm)
    return pl.pallas_call(
        _final_norm_kernel,
        grid=(grp.m // tm,),
        in_specs=[pl.BlockSpec((tm, d), lambda i: (i, 0)), pl.BlockSpec((1, d), lambda i: (0, 0))],
        out_specs=pl.BlockSpec((tm, d), lambda i: (i, 0)),
        out_shape=jax.ShapeDtypeStruct((grp.m, d), F32),
        compiler_params=_params("parallel"),
        name="final_norm",
    )(x, g.reshape(1, d))


def _mm_kernel(a_ref, w_ref, o_ref):
    o_ref[...] = _dot(a_ref[...], w_ref[...].astype(BF16)).astype(o_ref.dtype)


def _mm(a, w, tm, tn, out_dtype=F32, name="matmul"):
    m, k = a.shape
    n = w.shape[1]
    return pl.pallas_call(
        _mm_kernel,
        grid=(m // tm, n // tn),
        in_specs=[pl.BlockSpec((tm, k), lambda i, j: (i, 0)),
                  pl.BlockSpec((k, tn), lambda i, j: (0, j))],
        out_specs=pl.BlockSpec((tm, tn), lambda i, j: (i, j)),
        out_shape=jax.ShapeDtypeStruct((m, n), out_dtype),
        compiler_params=_params("parallel", "arbitrary"),
        name=name,
    )(a, w)


def _gateup_kernel(a_ref, wg_ref, wu_ref, o_ref):
    a = a_ref[...]
    g = _dot(a, wg_ref[...].astype(BF16))
    u = _dot(a, wu_ref[...].astype(BF16))
    o_ref[...] = (_silu(g) * u).astype(o_ref.dtype)


def _gateup(a, wg, wu, tm, tn):
    m, k = a.shape
    n = wg.shape[1]
    return pl.pallas_call(
        _gateup_kernel,
        grid=(m // tm, n // tn),
        in_specs=[pl.BlockSpec((tm, k), lambda i, j: (i, 0)),
                  pl.BlockSpec((k, tn), lambda i, j: (0, j)),
                  pl.BlockSpec((k, tn), lambda i, j: (0, j))],
        out_specs=pl.BlockSpec((tm, tn), lambda i, j: (i, j)),
        out_shape=jax.ShapeDtypeStruct((m, n), BF16),
        compiler_params=_params("parallel", "arbitrary"),
        name="ffn_gate_up",
    )(a, wg, wu)


def _mm_res_kernel(a_ref, w_ref, x_ref, g_ref, o_ref, *, coef):
    acc = _dot(a_ref[...], w_ref[...].astype(BF16))
    o_ref[...] = x_ref[...] + coef * g_ref[...] * acc


def _mm_res(grp, a, w, x, layer, w_gate, coef, tm, tn, name):
    m, k = a.shape
    n = w.shape[1]
    sub = _Group(grp.batch, grp.seq, grp.mods, tm)
    g_arr, g_spec = sub.mod(layer, w_gate, tn, True)
    return pl.pallas_call(
        functools.partial(_mm_res_kernel, coef=coef),
        grid=(m // tm, n // tn),
        in_specs=[pl.BlockSpec((tm, k), lambda i, j: (i, 0)),
                  pl.BlockSpec((k, tn), lambda i, j: (0, j)),
                  pl.BlockSpec((tm, tn), lambda i, j: (i, j)),
                  g_spec],
        out_specs=pl.BlockSpec((tm, tn), lambda i, j: (i, j)),
        out_shape=jax.ShapeDtypeStruct((m, n), F32),
        compiler_params=_params("parallel", "arbitrary"),
        name=name,
    )(a, w, x, g_arr)


def _merge_kernel(a0_ref, a1_ref, a2_ref, w0_ref, w1_ref, w2_ref, g0_ref, g1_ref, g2_ref, o_ref):
    acc = jax.nn.sigmoid(g0_ref[...]) * _dot(a0_ref[...], w0_ref[...].astype(BF16))
    acc += jax.nn.sigmoid(g1_ref[...]) * _dot(a1_ref[...], w1_ref[...].astype(BF16))
    acc += jax.nn.sigmoid(g2_ref[...]) * _dot(a2_ref[...], w2_ref[...].astype(BF16))
    o_ref[...] = acc.astype(o_ref.dtype)


def _merge(branches, w_branch_l, gate_logits, tm, tn):
    m, d = branches[0].shape
    nj = d // tn
    a_spec = pl.BlockSpec((tm, d), lambda i, j: (i, 0))
    w_specs = [pl.BlockSpec((None, d, tn), functools.partial(lambda i, j, br: (br, 0, j), br=br))
               for br in range(N_BRANCH)]
    g_specs = [pl.BlockSpec((tm, tn), functools.partial(lambda i, j, br: (i, br * nj + j), br=br))
               for br in range(N_BRANCH)]
    return pl.pallas_call(
        _merge_kernel,
        grid=(m // tm, nj),
        in_specs=[a_spec, a_spec, a_spec] + w_specs + g_specs,
        out_specs=pl.BlockSpec((tm, tn), lambda i, j: (i, j)),
        out_shape=jax.ShapeDtypeStruct((m, d), BF16),
        compiler_params=_params("parallel", "arbitrary"),
        name="branch_merge",
    )(*branches, w_branch_l, w_branch_l, w_branch_l, gate_logits, gate_logits, gate_logits)


def _rotary_kernel(att_ref, idx_ref, ca_ref, sa_ref, sb_ref, ci_ref, sia_ref, sib_ref,
                   q_ref, k_ref, v_ref, iq_ref, ik_ref, iw_ref, *, n_q, n_kv, n_iq):
    ca, sa, sb = ca_ref[...], sa_ref[...], sb_ref[...]
    ci, sia, sib = ci_ref[...], sia_ref[...], sib_ref[...]
    a_half = ATT_HEAD_DIM // ROPE_FRACTION // 2
    i_half = IDX_DIM // ROPE_FRACTION // 2

    def rot_a(x):
        return x * ca + pltpu.roll(x, LANES - a_half, 1) * sa + pltpu.roll(x, a_half, 1) * sb

    def rot_i(x):
        return x * ci + pltpu.roll(x, LANES - i_half, 1) * sia + pltpu.roll(x, i_half, 1) * sib

    for h in range(n_q):
        q_ref[:, h * LANES:(h + 1) * LANES] = rot_a(att_ref[:, h * LANES:(h + 1) * LANES]).astype(q_ref.dtype)
    for g in range(n_kv):
        k_ref[:, g * LANES:(g + 1) * LANES] = rot_a(att_ref[:, (n_q + g) * LANES:(n_q + g + 1) * LANES])
    v_ref[...] = att_ref[:, (n_q + n_kv) * LANES:(n_q + 2 * n_kv) * LANES]
    for j in range(n_iq):
        iq_ref[:, j * LANES:(j + 1) * LANES] = rot_i(idx_ref[:, j * LANES:(j + 1) * LANES])
    ik_ref[...] = rot_i(idx_ref[:, n_iq * LANES:(n_iq + 1) * LANES])[:, :IDX_DIM]
    iw_ref[...] = idx_ref[:, (n_iq + 1) * LANES:(n_iq + 2) * LANES] * (IDX_HEADS ** -0.5)


def _rotary(att, idx, tabs_a, tabs_i, tm, d_model):
    m = att.shape[0]
    p_rows = tabs_a[0].shape[0]
    n_q = d_model // ATT_HEAD_DIM
    n_kv = ATT_KV_HEADS
    n_iq = IDX_HEADS * IDX_DIM // LANES
    n_tab = p_rows // tm
    tab_spec = pl.BlockSpec((tm, LANES), lambda i: (i % n_tab, 0))
    kv_w = n_kv * ATT_HEAD_DIM
    return pl.pallas_call(
        functools.partial(_rotary_kernel, n_q=n_q, n_kv=n_kv, n_iq=n_iq),
        grid=(m // tm,),
        in_specs=[pl.BlockSpec((tm, att.shape[1]), lambda i: (i, 0)),
                  pl.BlockSpec((tm, idx.shape[1]), lambda i: (i, 0))] + [tab_spec] * 6,
        out_specs=[pl.BlockSpec((tm, d_model), lambda i: (i, 0)),
                   pl.BlockSpec((tm, kv_w), lambda i: (i, 0)),
                   pl.BlockSpec((tm, kv_w), lambda i: (i, 0)),
                   pl.BlockSpec((tm, n_iq * LANES), lambda i: (i, 0)),
                   pl.BlockSpec((tm, IDX_DIM), lambda i: (i, 0)),
                   pl.BlockSpec((tm, LANES), lambda i: (i, 0))],
        out_shape=[jax.ShapeDtypeStruct((m, d_model), BF16),
                   jax.ShapeDtypeStruct((m, kv_w), F32),
                   jax.ShapeDtypeStruct((m, kv_w), F32),
                   jax.ShapeDtypeStruct((m, n_iq * LANES), F32),
                   jax.ShapeDtypeStruct((m, IDX_DIM), F32),
                   jax.ShapeDtypeStruct((m, LANES), F32)],
        compiler_params=_params("parallel"),
        name="rotary",
    )(att, idx, *tabs_a, *tabs_i)


def _count_ge(s, t):
    return jnp.sum(jnp.where(s >= t, 1.0, 0.0), axis=-1, keepdims=True)


def _select_topk(score, adm, k):
    rows, keys = score.shape
    kf = float(k)
    s = jnp.where(adm, score, -jnp.inf)
    adm_f = jnp.where(adm, 1.0, 0.0)
    n_adm = jnp.sum(adm_f, axis=-1, keepdims=True)
    rmax = jnp.max(s, axis=-1, keepdims=True)
    rmin = jnp.min(jnp.where(adm, score, jnp.inf), axis=-1, keepdims=True)
    c_top = _count_ge(s, rmax)
    top_tie = c_top >= kf
    lo0 = jnp.where(top_tie, rmax, rmin)
    hi0 = jnp.where(top_tie, jnp.inf, rmax)
    c_lo0 = jnp.where(top_tie, c_top, n_adm)
    c_hi0 = jnp.where(top_tie, 0.0, c_top)
    done0 = jnp.where(top_tie | (n_adm <= kf), 1.0, 0.0)

    def cond(st):
        return jnp.min(st[4]) < 0.5

    def body(st):
        lo, hi, c_lo, c_hi, done = st
        mid = 0.5 * lo + 0.5 * hi
        c = _count_ge(s, mid)
        adjacent = (mid <= lo) | (mid >= hi)
        upd = (done < 0.5) & jnp.logical_not(adjacent)
        up_lo = upd & (c >= kf)
        up_hi = upd & (c < kf)
        lo = jnp.where(up_lo, mid, lo)
        c_lo = jnp.where(up_lo, c, c_lo)
        hi = jnp.where(up_hi, mid, hi)
        c_hi = jnp.where(up_hi, c, c_hi)
        done = jnp.where(adjacent | (c_lo == kf), 1.0, done)
        return lo, hi, c_lo, c_hi, done

    lo, hi, c_lo, c_hi, _ = lax.while_loop(cond, body, (lo0, hi0, c_lo0, c_hi0, done0))

    small = n_adm <= kf
    tied = jnp.logical_not(small) & (c_lo > kf)
    any_tied = jnp.max(jnp.where(tied, 1.0, 0.0)) > 0.5

    def plain():
        return jnp.where(s >= lo, 1.0, 0.0)

    def ranked():
        sure = s >= hi
        band = jnp.where((s >= lo) & jnp.logical_not(sure), 1.0, 0.0)
        quota = kf - c_hi
        tri = jnp.where(_iota((LANES, LANES), 0) <= _iota((LANES, LANES), 1), 1.0, 0.0).astype(BF16)
        off = jnp.zeros((rows, 1), F32)
        pieces = []
        for blk in range(keys // LANES):
            bb = band[:, blk * LANES:(blk + 1) * LANES]
            rank = _dot(bb.astype(BF16), tri) + off
            pieces.append(jnp.where((bb > 0.5) & (rank <= quota), 1.0, 0.0))
            off = off + jnp.sum(bb, axis=-1, keepdims=True)
        keep = jnp.concatenate(pieces, axis=-1)
        return jnp.where(sure, 1.0, keep)

    sel = lax.cond(any_tied, ranked, plain)
    return jnp.where(small, adm_f, sel)


def _dsa_prompt_kernel(q_ref, k_ref, v_ref, iq_ref, ik_ref, iw_ref, o_ref,
                       kb_ref, vb_ref, ikh_ref, ikl_ref, *, k_top, n_heads):
    qi = pl.program_id(1)
    tq = q_ref.shape[0]
    seq = k_ref.shape[0]
    q_per_kv = n_heads // ATT_KV_HEADS

    @pl.when(qi == 0)
    def _():
        kb_ref[...] = k_ref[...].astype(BF16)
        vb_ref[...] = v_ref[...].astype(BF16)
        hi, lo = _split2(ik_ref[...])
        ikh_ref[...] = hi
        ikl_ref[...] = lo

    ikh, ikl = ikh_ref[...], ikl_ref[...]
    iw = iw_ref[...]
    score = jnp.zeros((tq, seq), F32)
    for h in range(IDX_HEADS):
        a_hi, a_lo = _split2(iq_ref[:, h * IDX_DIM:(h + 1) * IDX_DIM])
        s = _dot_nt(a_hi, ikh) + _dot_nt(a_hi, ikl) + _dot_nt(a_lo, ikh)
        score = score + jnp.maximum(s * (IDX_DIM ** -0.5), 0.0) * iw[:, h:h + 1]

    q_pos = qi * tq + _iota((tq, seq), 0)
    adm = _iota((tq, seq), 1) <= q_pos
    sel = _select_topk(score, adm, k_top) > 0.5

    for g in range(ATT_KV_HEADS):
        kg = kb_ref[:, g * ATT_HEAD_DIM:(g + 1) * ATT_HEAD_DIM]
        vg = vb_ref[:, g * ATT_HEAD_DIM:(g + 1) * ATT_HEAD_DIM]
        for hh in range(q_per_kv):
            h = g * q_per_kv + hh
            logits = _dot_nt(q_ref[:, h * ATT_HEAD_DIM:(h + 1) * ATT_HEAD_DIM], kg) * (ATT_HEAD_DIM ** -0.5)
            logits = jnp.where(sel, logits, -jnp.inf)
            mx = jnp.max(logits, axis=-1, keepdims=True)
            p = jnp.exp(logits - mx)
            den = jnp.sum(p, axis=-1, keepdims=True)
            o = _dot(p.astype(BF16), vg) / den
            o_ref[:, h * ATT_HEAD_DIM:(h + 1) * ATT_HEAD_DIM] = o.astype(o_ref.dtype)


def _dsa_prompt(q, k, v, iq, ik, iw, batch, seq, d_model):
    tq = 128
    nq = seq // tq
    k_top = min(TOPK_MAX, seq // 4)
    kv_w = ATT_KV_HEADS * ATT_HEAD_DIM
    return pl.pallas_call(
        functools.partial(_dsa_prompt_kernel, k_top=k_top, n_heads=d_model // ATT_HEAD_DIM),
        grid=(batch, nq),
        in_specs=[pl.BlockSpec((tq, d_model), lambda b, i: (b * nq + i, 0)),
                  pl.BlockSpec((seq, kv_w), lambda b, i: (b, 0)),
                  pl.BlockSpec((seq, kv_w), lambda b, i: (b, 0)),
                  pl.BlockSpec((tq, IDX_HEADS * IDX_DIM), lambda b, i: (b * nq + i, 0)),
                  pl.BlockSpec((seq, IDX_DIM), lambda b, i: (b, 0)),
                  pl.BlockSpec((tq, LANES), lambda b, i: (b * nq + i, 0))],
        out_specs=pl.BlockSpec((tq, d_model), lambda b, i: (b * nq + i, 0)),
        out_shape=jax.ShapeDtypeStruct((batch * seq, d_model), BF16),
        scratch_shapes=[pltpu.VMEM((seq, kv_w), BF16), pltpu.VMEM((seq, kv_w), BF16),
                        pltpu.VMEM((seq, IDX_DIM), BF16), pltpu.VMEM((seq, IDX_DIM), BF16)],
        compiler_params=_params("parallel", "arbitrary"),
        name="dsa_prompt",
    )(q, k, v, iq, ik, iw)


def _dsa_score_kernel(pt_ref, iq_ref, iw_ref, kpage_ref, o_ref):
    a_hi, a_lo = _split2(iq_ref[...])
    b_hi, b_lo = _split2(kpage_ref[...])
    s = _dot_nt(a_hi, b_hi) + _dot_nt(a_hi, b_lo) + _dot_nt(a_lo, b_hi)
    s = jnp.maximum(s * (IDX_DIM ** -0.5), 0.0) * iw_ref[...]
    o_ref[...] = jnp.sum(s, axis=0, keepdims=True)


def _dsa_scores(page_table, iq, iw, cache_idx_k, layer):
    bd, n_pages = page_table.shape
    page = cache_idx_k.shape[2]
    grid_spec = pltpu.PrefetchScalarGridSpec(
        num_scalar_prefetch=1,
        grid=(bd, n_pages),
        in_specs=[pl.BlockSpec((None, IDX_HEADS, IDX_DIM), lambda b, p, pt: (b, 0, 0)),
                  pl.BlockSpec((None, IDX_HEADS, 1), lambda b, p, pt: (b, 0, 0)),
                  pl.BlockSpec((None, None, page, IDX_DIM), lambda b, p, pt: (pt[b, p], layer, 0, 0))],
        out_specs=pl.BlockSpec((None, None, 1, page), lambda b, p, pt: (b, p, 0, 0)),
    )
    out = pl.pallas_call(
        _dsa_score_kernel,
        grid_spec=grid_spec,
        out_shape=jax.ShapeDtypeStruct((bd, n_pages, 1, page), F32),
        compiler_params=_params("parallel", "arbitrary"),
        name="dsa_sample_scores",
    )(page_table, iq.reshape(bd, IDX_HEADS, IDX_DIM), iw[:, :IDX_HEADS].reshape(bd, IDX_HEADS, 1), cache_idx_k)
    return out.reshape(bd, n_pages * page)


def _dsa_sample_select_kernel(sc_ref, iq_ref, ik_ref, iw_ref, o_ref, *, k_top):
    bd, past = sc_ref.shape
    iw = iw_ref[...]
    ik = ik_ref[...]
    s_new = jnp.zeros((bd, 1), F32)
    for h in range(IDX_HEADS):
        d = jnp.sum(iq_ref[:, h * IDX_DIM:(h + 1) * IDX_DIM] * ik, axis=-1, keepdims=True)
        s_new = s_new + jnp.maximum(d * (IDX_DIM ** -0.5), 0.0) * iw[:, h:h + 1]
    lane = _iota((bd, LANES), 1)
    tail = jnp.where(lane == 0, s_new, 0.0)
    score = jnp.concatenate([sc_ref[...], tail], axis=-1)
    adm = _iota((bd, past + LANES), 1) <= past
    o_ref[...] = _select_topk(score, adm, k_top)


def _dsa_sample_select(scores, iq, ik, iw):
    bd, past = scores.shape
    k_top = min(TOPK_MAX, (past + 1) // 4)
    return pl.pallas_call(
        functools.partial(_dsa_sample_select_kernel, k_top=k_top),
        out_shape=jax.ShapeDtypeStruct((bd, past + LANES), F32),
        compiler_params=pltpu.CompilerParams(vmem_limit_bytes=VMEM_LIMIT_BYTES),
        name="dsa_sample_select",
    )(scores, iq, ik, iw)


def _dsa_sample_attend_kernel(pt_ref, q_ref, kpage_ref, vpage_ref, sel_ref, knew_ref, vnew_ref, o_ref,
                              m_ref, l_ref, acc_ref):
    p = pl.program_id(1)
    n_pages = pl.num_programs(1) - 1
    n_heads = q_ref.shape[0]
    q_per_kv = n_heads // ATT_KV_HEADS
    scale = ATT_HEAD_DIM ** -0.5
    head_group = _iota((n_heads, LANES), 0) // q_per_kv

    @pl.when(p == 0)
    def _():
        m_ref[...] = jnp.full(m_ref.shape, -1e30, F32)
        l_ref[...] = jnp.zeros(l_ref.shape, F32)
        acc_ref[...] = jnp.zeros(acc_ref.shape, F32)

    def update(logits, sel, pv_fn):
        logits = jnp.where(sel, logits, -1e30)
        m_old = m_ref[...]
        m_new = jnp.maximum(m_old, jnp.max(logits, axis=-1, keepdims=True))
        alpha = jnp.exp(m_old - m_new)
        pr = jnp.where(sel, jnp.exp(logits - m_new), 0.0)
        l_ref[...] = alpha * l_ref[...] + jnp.sum(pr, axis=-1, keepdims=True)
        acc_ref[...] = alpha * acc_ref[...] + pv_fn(pr)
        m_ref[...] = m_new

    @pl.when(p < n_pages)
    def _():
        q = q_ref[...]
        logits = jnp.zeros((n_heads, LANES), F32)
        for g in range(ATT_KV_HEADS):
            kg = kpage_ref[:, g * ATT_HEAD_DIM:(g + 1) * ATT_HEAD_DIM].astype(BF16)
            logits = jnp.where(head_group == g, _dot_nt(q, kg), logits)
        sel = jnp.broadcast_to(sel_ref[...] > 0.5, (n_heads, LANES))

        def pv(pr):
            prb = pr.astype(BF16)
            out = jnp.zeros((n_heads, ATT_HEAD_DIM), F32)
            for g in range(ATT_KV_HEADS):
                vg = vpage_ref[:, g * ATT_HEAD_DIM:(g + 1) * ATT_HEAD_DIM].astype(BF16)
                out = jnp.where(head_group == g, _dot(prb, vg), out)
            return out

        update(logits * scale, sel, pv)

    @pl.when(p == n_pages)
    def _():
        qf = q_ref[...].astype(F32)
        kexp = jnp.zeros((n_heads, ATT_HEAD_DIM), F32)
        vexp = jnp.zeros((n_heads, ATT_HEAD_DIM), F32)
        for g in range(ATT_KV_HEADS):
            kexp = jnp.where(head_group == g, knew_ref[:, g * ATT_HEAD_DIM:(g + 1) * ATT_HEAD_DIM], kexp)
            vexp = jnp.where(head_group == g, vnew_ref[:, g * ATT_HEAD_DIM:(g + 1) * ATT_HEAD_DIM], vexp)
        lg = jnp.sum(qf * kexp, axis=-1, keepdims=True) * scale
        sel_new = sel_ref[:, 0:1] > 0.5
        lane0 = _iota((n_heads, LANES), 1) == 0
        logits = jnp.broadcast_to(lg, (n_heads, LANES))
        sel = lane0 & jnp.broadcast_to(sel_new, (n_heads, LANES))
        update(logits, sel, lambda pr: jnp.sum(pr, axis=-1, keepdims=True) * vexp)
        o_ref[...] = (acc_ref[...] / l_ref[...]).astype(o_ref.dtype)


def _dsa_sample_attend(page_table, q, cache_k, cache_v, sel, k_new, v_new, layer, d_model):
    bd, n_pages = page_table.shape
    page = cache_k.shape[2]
    n_heads = d_model // ATT_HEAD_DIM
    kv_w = ATT_KV_HEADS * ATT_HEAD_DIM
    last = n_pages - 1
    page_map = lambda b, p, pt: (pt[b, jnp.minimum(p, last)], layer, 0, 0)
    grid_spec = pltpu.PrefetchScalarGridSpec(
        num_scalar_prefetch=1,
        grid=(bd, n_pages + 1),
        in_specs=[pl.BlockSpec((None, n_heads, ATT_HEAD_DIM), lambda b, p, pt: (b, 0, 0)),
                  pl.BlockSpec((None, None, page, kv_w), page_map),
                  pl.BlockSpec((None, None, page, kv_w), page_map),
                  pl.BlockSpec((None, None, 1, page), lambda b, p, pt: (b, p, 0, 0)),
                  pl.BlockSpec((None, 1, kv_w), lambda b, p, pt: (b, 0, 0)),
                  pl.BlockSpec((None, 1, kv_w), lambda b, p, pt: (b, 0, 0))],
        out_specs=pl.BlockSpec((None, n_heads, ATT_HEAD_DIM), lambda b, p, pt: (b, 0, 0)),
        scratch_shapes=[pltpu.VMEM((n_heads, 1), F32), pltpu.VMEM((n_heads, 1), F32),
                        pltpu.VMEM((n_heads, ATT_HEAD_DIM), F32)],
    )
    n_pool, depth = cache_k.shape[:2]
    out = pl.pallas_call(
        _dsa_sample_attend_kernel,
        grid_spec=grid_spec,
        out_shape=jax.ShapeDtypeStruct((bd, n_heads, ATT_HEAD_DIM), BF16),
        compiler_params=_params("parallel", "arbitrary"),
        name="dsa_sample_attend",
    )(page_table, q.reshape(bd, n_heads, ATT_HEAD_DIM),
      cache_k.reshape(n_pool, depth, page, kv_w), cache_v.reshape(n_pool, depth, page, kv_w),
      sel.reshape(bd, n_pages + 1, 1, page), k_new.reshape(bd, 1, kv_w), v_new.reshape(bd, 1, kv_w))
    return out.reshape(bd, d_model)


def _ssd_kernel(z_ref, xs_ref, bc_ref, dt_ref, cw_ref, cb_ref, alog_ref, dtb_ref, dfull_ref, ng_ref,
                e_ref, et_ref, conv0_ref, s0_ref, y_ref, convn_ref, sn_ref, ext_ref, st_ref, yb_ref):
    c = pl.program_id(1)
    nc = pl.num_programs(1)
    q = xs_ref.shape[0]
    inner = xs_ref.shape[1]
    gw = inner // SSM_GROUPS
    heads_pg = gw // SSM_HEAD_DIM
    n_heads = inner // SSM_HEAD_DIM
    pad = 8
    tail = CONV_WIDTH - 1

    @pl.when(c == 0)
    def _():
        ext_ref[0:pad, :] = conv0_ref[...]
        st_ref[...] = s0_ref[...]

    ext_ref[pad:pad + q, 0:inner] = xs_ref[...]
    ext_ref[pad:pad + q, inner:] = bc_ref[...]
    conv = cb_ref[...] + cw_ref[0:1, :] * ext_ref[pl.ds(pad - 3, q), :]
    for i in range(1, CONV_WIDTH):
        conv = conv + cw_ref[i:i + 1, :] * ext_ref[pl.ds(pad - 3 + i, q), :]

    @pl.when(c == nc - 1)
    def _():
        convn_ref[...] = ext_ref[pl.ds(pad + q - tail, tail), :]

    ext_ref[0:pad, :] = ext_ref[q:q + pad, :]
    act = _silu(conv)
    xs = act[:, :inner]
    bm = act[:, inner:inner + SSM_GROUPS * SSM_STATE].astype(BF16)
    cm = act[:, inner + SSM_GROUPS * SSM_STATE:].astype(BF16)

    dt = _softplus(dt_ref[...] + dtb_ref[...])
    la = dt * (-jnp.exp(alog_ref[...]))
    tril = jnp.where(_iota((q, q), 1) <= _iota((q, q), 0), 1.0, 0.0).astype(BF16)
    acs = _sel_left(tril, la)
    acs_t = acs.T
    e = e_ref[...]
    acs_full = _sel_right(acs, e)
    dt_full = _sel_right(dt, e)
    exp_acs = jnp.exp(acs_full)
    to_end = jnp.exp(acs_full[q - 1:q, :] - acs_full)
    v = xs * dt_full
    vb = v.astype(BF16)
    vte = (v * to_end).astype(BF16)
    causal = _iota((q, q), 1) <= _iota((q, q), 0)
    lane_lo = _iota((q, LANES), 1) < SSM_HEAD_DIM
    zero_b = jnp.zeros((q, LANES), BF16)

    a_last = jnp.broadcast_to(acs_t[:, q - 1:q], (LANES, LANES))
    decay_rows = jnp.exp(_sel_left(et_ref[...], a_last))

    for g in range(SSM_GROUPS):
        bg = bm[:, g * SSM_STATE:(g + 1) * SSM_STATE]
        cg = cm[:, g * SSM_STATE:(g + 1) * SSM_STATE]
        qk = _dot_nt(cg, bg)
        rows = slice(g * gw, (g + 1) * gw)
        s_old = st_ref[rows, :]
        y_off = _dot_nt(cg, s_old.astype(BF16))
        new_state = _dot_tn(vte[:, rows], bg)
        st_ref[rows, :] = s_old * decay_rows[rows, :] + new_state
        yb_ref[:, rows] = y_off * exp_acs[:, rows]
        for pr in range(heads_pg // 2):
            h0 = g * heads_pg + 2 * pr
            ms = []
            for h in (h0, h0 + 1):
                seg = acs[:, h:h + 1] - acs_t[h:h + 1, :]
                ms.append((qk * jnp.exp(jnp.where(causal, seg, -jnp.inf))).astype(BF16))
            m2 = jnp.concatenate(ms, axis=1)
            cols = slice(h0 * SSM_HEAD_DIM, (h0 + 2) * SSM_HEAD_DIM)
            vp = vb[:, cols]
            v2 = jnp.concatenate([jnp.where(lane_lo, vp, zero_b), jnp.where(lane_lo, zero_b, vp)], axis=0)
            yb_ref[:, cols] += _dot(m2, v2)

    y = (yb_ref[...] + xs * dfull_ref[...]) * _silu(z_ref[...])
    for g in range(SSM_GROUPS):
        yg = y[:, g * gw:(g + 1) * gw]
        yn = yg * lax.rsqrt(jnp.mean(yg * yg, axis=-1, keepdims=True) + NORM_EPS)
        y_ref[:, g * gw:(g + 1) * gw] = (yn * ng_ref[:, g * gw:(g + 1) * gw]).astype(y_ref.dtype)

    @pl.when(c == nc - 1)
    def _():
        sn_ref[...] = st_ref[...]


def _ssd_prompt(zx, dt, wts, conv0p, s0, batch, seq, inner):
    q = SCAN_CHUNK
    nc = seq // q
    conv_ch = inner + 2 * SSM_GROUPS * SSM_STATE
    bc_w = conv_ch - inner
    full = lambda shape: pl.BlockSpec(shape, lambda b, c: (0,) * len(shape))
    return pl.pallas_call(
        _ssd_kernel,
        grid=(batch, nc),
        in_specs=[pl.BlockSpec((q, inner), lambda b, c: (b * nc + c, 0)),
                  pl.BlockSpec((q, inner), lambda b, c: (b * nc + c, 1)),
                  pl.BlockSpec((q, bc_w), lambda b, c: (b * nc + c, 2 * inner // bc_w)),
                  pl.BlockSpec((q, LANES), lambda b, c: (b * nc + c, 0)),
                  full((CONV_WIDTH, conv_ch)), full((1, conv_ch)), full((1, LANES)), full((1, LANES)),
                  full((1, inner)), full((1, inner)), full((LANES, inner)), full((inner, LANES)),
                  pl.BlockSpec((None, 8, conv_ch), lambda b, c: (b, 0, 0)),
                  pl.BlockSpec((None, inner, SSM_STATE), lambda b, c: (b, 0, 0))],
        out_specs=[pl.BlockSpec((q, inner), lambda b, c: (b * nc + c, 0)),
                   pl.BlockSpec((None, CONV_WIDTH - 1, conv_ch), lambda b, c: (b, 0, 0)),
                   pl.BlockSpec((None, inner, SSM_STATE), lambda b, c: (b, 0, 0))],
        out_shape=[jax.ShapeDtypeStruct((batch * seq, inner), BF16),
                   jax.ShapeDtypeStruct((batch, CONV_WIDTH - 1, conv_ch), F32),
                   jax.ShapeDtypeStruct((batch, inner, SSM_STATE), F32)],
        scratch_shapes=[pltpu.VMEM((q + 8, conv_ch), F32), pltpu.VMEM((inner, SSM_STATE), F32),
                        pltpu.VMEM((q, inner), F32)],
        compiler_params=_params("parallel", "arbitrary"),
        name="ssd_prompt",
    )(zx, zx, zx, dt, wts["conv_w"], wts["conv_b"], wts["a_log"], wts["dt_bias"], wts["d_full"],
      wts["ssm_norm"], wts["e"], wts["et"], conv0p, s0)


def _ret_kernel(rq_ref, rk_ref, rv_ref, rg_ref, cos_ref, sin_ref, dmat_ref, rin_ref, tend_ref, cd_ref,
                s0_ref, y_ref, sn_ref, st_ref):
    c = pl.program_id(1)
    nc = pl.num_programs(1)
    kd = st_ref.shape[2]
    vd = st_ref.shape[1]

    @pl.when(c == 0)
    def _():
        st_ref[...] = s0_ref[...]

    cos, sin = cos_ref[...], sin_ref[...]

    def rot(x):
        return x * cos + pltpu.roll(x, kd // 2, 1) * sin

    for h in range(RET_HEADS):
        qr = rot(rq_ref[:, h * kd:(h + 1) * kd]).astype(BF16)
        kr = (rot(rk_ref[:, h * kd:(h + 1) * kd]) * (kd ** -0.5)).astype(BF16)
        vh = rv_ref[:, h * vd:(h + 1) * vd]
        m = (_dot_nt(qr, kr) * dmat_ref[h]).astype(BF16)
        s_old = st_ref[h]
        y = _dot(m, vh.astype(BF16)) + _dot_nt(qr, s_old.astype(BF16)) * rin_ref[h]
        st_ref[h] = s_old * cd_ref[h] + _dot_tn((vh * tend_ref[h]).astype(BF16), kr)
        yn = y * lax.rsqrt(jnp.mean(y * y, axis=-1, keepdims=True) + NORM_EPS)
        y_ref[:, h * vd:(h + 1) * vd] = (_silu(rg_ref[:, h * vd:(h + 1) * vd]) * yn).astype(y_ref.dtype)

    @pl.when(c == nc - 1)
    def _():
        sn_ref[...] = st_ref[...]


def _ret_prompt(ret, tabs, consts, s0, batch, seq, d_model):
    q = SCAN_CHUNK
    nc = seq // q
    vd = d_model // RET_HEADS
    kd = vd // 2
    qk_w = RET_HEADS * kd
    full = lambda shape: pl.BlockSpec(shape, lambda b, c: (0,) * len(shape))
    return pl.pallas_call(
        _ret_kernel,
        grid=(batch, nc),
        in_specs=[pl.BlockSpec((q, qk_w), lambda b, c: (b * nc + c, 0)),
                  pl.BlockSpec((q, qk_w), lambda b, c: (b * nc + c, 1)),
                  pl.BlockSpec((q, d_model), lambda b, c: (b * nc + c, 1)),
                  pl.BlockSpec((q, d_model), lambda b, c: (b * nc + c, 2)),
                  pl.BlockSpec((q, kd), lambda b, c: (c, 0)),
                  pl.BlockSpec((q, kd), lambda b, c: (c, 0)),
                  full((RET_HEADS, q, q)), full((RET_HEADS, q, 1)), full((RET_HEADS, q, 1)),
                  full((RET_HEADS, 1, kd)),
                  pl.BlockSpec((None, RET_HEADS, vd, kd), lambda b, c: (b, 0, 0, 0))],
        out_specs=[pl.BlockSpec((q, d_model), lambda b, c: (b * nc + c, 0)),
                   pl.BlockSpec((None, RET_HEADS, vd, kd), lambda b, c: (b, 0, 0, 0))],
        out_shape=[jax.ShapeDtypeStruct((batch * seq, d_model), BF16),
                   jax.ShapeDtypeStruct((batch, RET_HEADS, vd, kd), F32)],
        scratch_shapes=[pltpu.VMEM((RET_HEADS, vd, kd), F32)],
        compiler_params=_params("parallel", "arbitrary"),
        name="ret_prompt",
    )(ret, ret, ret, ret, tabs[0], tabs[1], consts["dmat"], consts["rin"], consts["tend"], consts["cd"], s0)


def _ssd_pre_kernel(zx_ref, dt_ref, cbuf_ref, cw_ref, cb_ref, alog_ref, dtb_ref, e_ref,
                    convn_ref, v_ref, da_ref, xs_ref, bm_ref, cm_ref, *, inner):
    x = zx_ref[:, inner:]
    conv = cb_ref[...] + cw_ref[CONV_WIDTH - 1:CONV_WIDTH, :] * x
    for i in range(CONV_WIDTH - 1):
        conv = conv + cw_ref[i:i + 1, :] * cbuf_ref[i]
    for i in range(CONV_WIDTH - 2):
        convn_ref[i] = cbuf_ref[i + 1]
    convn_ref[CONV_WIDTH - 2] = x
    act = _silu(conv)
    xs = act[:, :inner]
    n_bc = SSM_GROUPS * SSM_STATE
    dt = _softplus(dt_ref[...] + dtb_ref[...])
    la = dt * (-jnp.exp(alog_ref[...]))
    e = e_ref[...]
    v_ref[...] = xs * _sel_right(dt, e)
    da_ref[...] = jnp.exp(_sel_right(la, e))
    xs_ref[...] = xs
    bm_ref[...] = act[:, inner:inner + n_bc]
    cm_ref[...] = act[:, inner + n_bc:]


def _ssd_pre(zx, dt, cbuf, wts, inner):
    bd = zx.shape[0]
    conv_ch = inner + 2 * SSM_GROUPS * SSM_STATE
    n_bc = SSM_GROUPS * SSM_STATE
    sd = jax.ShapeDtypeStruct
    return pl.pallas_call(
        functools.partial(_ssd_pre_kernel, inner=inner),
        out_shape=[sd((CONV_WIDTH - 1, bd, conv_ch), F32), sd((bd, inner), F32), sd((bd, inner), F32),
                   sd((bd, inner), F32), sd((bd, n_bc), F32), sd((bd, n_bc), F32)],
        compiler_params=pltpu.CompilerParams(vmem_limit_bytes=VMEM_LIMIT_BYTES),
        name="ssd_step_pre",
    )(zx, dt, cbuf, wts["conv_w"], wts["conv_b"], wts["a_log"], wts["dt_bias"], wts["e"])


def _state_step_kernel(v_ref, d_ref, k_ref, q_ref, s_ref, y_ref, sn_ref, vt_ref, dt_ref, pad_ref):
    b = pl.program_id(0)
    bd, rows = v_ref.shape
    n_groups = k_ref.shape[0]
    gr = rows // n_groups

    @pl.when(b == 0)
    def _():
        pad_ref[...] = jnp.zeros(pad_ref.shape, F32)
        for src, dst in ((v_ref, vt_ref), (d_ref, dt_ref)):
            pad_ref[0:bd, :] = src[...]
            for t in range(rows // LANES):
                dst[t * LANES:(t + 1) * LANES, :] = pad_ref[:, t * LANES:(t + 1) * LANES].T

    onehot = jnp.where(_iota((LANES, LANES), 0) == b, 1.0, 0.0).astype(BF16)
    v_rows = _sel_right(vt_ref[...], onehot)
    d_rows = _sel_right(dt_ref[...], onehot)
    for g in range(n_groups):
        rs = slice(g * gr, (g + 1) * gr)
        s_new = s_ref[rs, :] * d_rows[rs, :] + v_rows[rs, :] * k_ref[g:g + 1, :]
        sn_ref[rs, :] = s_new
        q8 = jnp.broadcast_to(q_ref[g:g + 1, :], (8, q_ref.shape[1])).astype(BF16)
        y_ref[pl.ds(b, 1), rs] = _dot_nt(q8, s_new.astype(BF16))[0:1, :]


def _state_step(v, d, k, q, state):
    bd, rows = v.shape
    n_groups, kd = k.shape[1], k.shape[2]
    return pl.pallas_call(
        _state_step_kernel,
        grid=(bd,),
        in_specs=[pl.BlockSpec((bd, rows), lambda b: (0, 0)),
                  pl.BlockSpec((bd, rows), lambda b: (0, 0)),
                  pl.BlockSpec((None, n_groups, kd), lambda b: (b, 0, 0)),
                  pl.BlockSpec((None, n_groups, kd), lambda b: (b, 0, 0)),
                  pl.BlockSpec((None, rows, kd), lambda b: (b, 0, 0))],
        out_specs=[pl.BlockSpec((bd, rows), lambda b: (0, 0)),
                   pl.BlockSpec((None, rows, kd), lambda b: (b, 0, 0))],
        out_shape=[jax.ShapeDtypeStruct((bd, rows), F32),
                   jax.ShapeDtypeStruct((bd, rows, kd), F32)],
        scratch_shapes=[pltpu.VMEM((rows, LANES), F32), pltpu.VMEM((rows, LANES), F32),
                        pltpu.VMEM((LANES, rows), F32)],
        compiler_params=_params("arbitrary"),
        name="state_step",
    )(v, d, k, q, state)


def _ssd_post_kernel(y_ref, xs_ref, zx_ref, dfull_ref, ng_ref, o_ref, *, inner):
    y = (y_ref[...] + xs_ref[...] * dfull_ref[...]) * _silu(zx_ref[:, :inner])
    gw = inner // SSM_GROUPS
    for g in range(SSM_GROUPS):
        yg = y[:, g * gw:(g + 1) * gw]
        yn = yg * lax.rsqrt(jnp.mean(yg * yg, axis=-1, keepdims=True) + NORM_EPS)
        o_ref[:, g * gw:(g + 1) * gw] = (yn * ng_ref[:, g * gw:(g + 1) * gw]).astype(o_ref.dtype)


def _ssd_post(y, xs, zx, wts, inner):
    return pl.pallas_call(
        functools.partial(_ssd_post_kernel, inner=inner),
        out_shape=jax.ShapeDtypeStruct(y.shape, BF16),
        compiler_params=pltpu.CompilerParams(vmem_limit_bytes=VMEM_LIMIT_BYTES),
        name="ssd_step_post",
    )(y, xs, zx, wts["d_full"], wts["ssm_norm"])


def _ret_pre_kernel(ret_ref, cos_ref, sin_ref, q_ref, k_ref, *, kd):
    cos, sin = cos_ref[...], sin_ref[...]
    qk_w = RET_HEADS * kd

    def rot(x):
        return x * cos + pltpu.roll(x, kd // 2, 1) * sin

    for h in range(RET_HEADS):
        q_ref[:, h * kd:(h + 1) * kd] = rot(ret_ref[:, h * kd:(h + 1) * kd])
        k_ref[:, h * kd:(h + 1) * kd] = rot(ret_ref[:, qk_w + h * kd:qk_w + (h + 1) * kd]) * (kd ** -0.5)


def _ret_pre(ret, tabs, kd):
    bd = ret.shape[0]
    sd = jax.ShapeDtypeStruct
    return pl.pallas_call(
        functools.partial(_ret_pre_kernel, kd=kd),
        out_shape=[sd((bd, RET_HEADS * kd), F32), sd((bd, RET_HEADS * kd), F32)],
        compiler_params=pltpu.CompilerParams(vmem_limit_bytes=VMEM_LIMIT_BYTES),
        name="ret_step_pre",
    )(ret, tabs[0], tabs[1])


def _ret_post_kernel(y_ref, ret_ref, o_ref, *, vd, g_off):
    for h in range(RET_HEADS):
        y = y_ref[:, h * vd:(h + 1) * vd]
        yn = y * lax.rsqrt(jnp.mean(y * y, axis=-1, keepdims=True) + NORM_EPS)
        g = ret_ref[:, g_off + h * vd:g_off + (h + 1) * vd]
        o_ref[:, h * vd:(h + 1) * vd] = (_silu(g) * yn).astype(o_ref.dtype)


def _ret_post(y, ret, vd, g_off):
    return pl.pallas_call(
        functools.partial(_ret_post_kernel, vd=vd, g_off=g_off),
        out_shape=jax.ShapeDtypeStruct(y.shape, BF16),
        compiler_params=pltpu.CompilerParams(vmem_limit_bytes=VMEM_LIMIT_BYTES),
        name="ret_step_post",
    )(y, ret)


def _rope_tables(pos, rot_dim, width, theta):
    half = rot_dim // 2
    inv_freq = theta ** (-jnp.arange(half, dtype=F32) * (2.0 / rot_dim))
    ang = pos.astype(F32)[:, None] * inv_freq[None, :]
    cos, sin = jnp.cos(ang), jnp.sin(ang)
    n = pos.shape[0]
    ones = jnp.ones((n, width - rot_dim), F32)
    zeros = jnp.zeros((n, width - rot_dim), F32)
    z_half = jnp.zeros((n, half), F32)
    c = jnp.concatenate([cos, cos, ones], axis=1)
    sa = jnp.concatenate([-sin, z_half, zeros], axis=1)
    sb = jnp.concatenate([z_half, sin, zeros], axis=1)
    rep = LANES // width
    return tuple(jnp.tile(t, (1, rep)) for t in (c, sa, sb))


def _ret_tables(pos, kd):
    half = kd // 2
    inv_freq = RET_THETA ** (-jnp.arange(half, dtype=F32) * (2.0 / kd))
    ang = pos.astype(F32)[:, None] * inv_freq[None, :]
    cos, sin = jnp.cos(ang), jnp.sin(ang)
    return jnp.concatenate([cos, cos], axis=1), jnp.concatenate([-sin, sin], axis=1)


def _ret_consts(q):
    lg = jnp.log(1.0 - 2.0 ** (-5.0 - jnp.arange(RET_HEADS, dtype=F32)))
    i = jnp.arange(q, dtype=F32)
    diff = i[:, None] - i[None, :]
    dmat = jnp.where(diff >= 0, jnp.exp(diff[None] * lg[:, None, None]), 0.0)
    rin = jnp.exp((i + 1.0)[None, :] * lg[:, None])[..., None]
    tend = jnp.exp((q - 1.0 - i)[None, :] * lg[:, None])[..., None]
    return lg, {"dmat": dmat, "rin": rin, "tend": tend}


def _pack_layer(l, w_in, conv_w, conv_b, ssm_a_log, ssm_dt_bias, ssm_d, ssm_norm, d_model):
    inner = d_model
    conv_ch = inner + 2 * SSM_GROUPS * SSM_STATE
    n_ssm_heads = inner // SSM_HEAD_DIM
    n_att = d_model
    kv_w = ATT_KV_HEADS * ATT_HEAD_DIM
    iq_w = IDX_HEADS * IDX_DIM
    vd = d_model // RET_HEADS
    kd = vd // 2
    widths = (inner, conv_ch, n_ssm_heads, n_att, kv_w, kv_w, iq_w, IDX_DIM, IDX_HEADS,
              RET_HEADS * kd, RET_HEADS * kd, RET_HEADS * vd, RET_HEADS * vd, N_BRANCH * d_model)
    offs = [0]
    for w in widths:
        offs.append(offs[-1] + w)
    w = w_in[l]
    col = lambda a, b: w[:, offs[a]:offs[b]]
    zpad = lambda n: jnp.zeros((w.shape[0], n), w.dtype)
    pad1 = lambda x: jnp.concatenate([x, jnp.zeros((LANES - x.shape[0],), x.dtype)]).reshape(1, LANES)
    head_of = jnp.arange(inner) // SSM_HEAD_DIM
    e = (jnp.arange(LANES)[:, None] == head_of[None, :]).astype(BF16)
    return {
        "w_ssd": col(0, 2),
        "w_dt": jnp.concatenate([col(2, 3), zpad(LANES - n_ssm_heads)], axis=1),
        "w_att": col(3, 6),
        "w_idx": jnp.concatenate([col(6, 8), zpad(LANES - IDX_DIM), col(8, 9), zpad(LANES - IDX_HEADS)], axis=1),
        "w_ret": col(9, 13),
        "w_gate": col(13, 14),
        "conv_w": conv_w[l], "conv_b": conv_b[l].reshape(1, conv_ch),
        "a_log": pad1(ssm_a_log[l]), "dt_bias": pad1(ssm_dt_bias[l]),
        "d_full": jnp.repeat(ssm_d[l], SSM_HEAD_DIM).reshape(1, inner),
        "ssm_norm": ssm_norm[l].reshape(1, inner),
        "e": e, "et": e.T,
    }


def _mixer_inputs(h, wts, tm):
    zx = _mm(h, wts["w_ssd"], tm, 512, name="in_proj_ssd")
    dt = _mm(h, wts["w_dt"], tm, LANES, name="in_proj_dt")
    att = _mm(h, wts["w_att"], tm, 512, name="in_proj_att")
    idx = _mm(h, wts["w_idx"], tm, 256, name="in_proj_idx")
    ret = _mm(h, wts["w_ret"], tm, 512, name="in_proj_ret")
    gl = _mm(h, wts["w_gate"], tm, 512, name="in_proj_gate")
    return zx, dt, att, idx, ret, gl


def _ffn(grp, x, l, norm_g_l, w_gate, w_up, w_down, which):
    d_ff = w_gate.shape[1]
    h = _normmod(grp, x, norm_g_l, l, which, which + 1)
    act = _gateup(h, w_gate, w_up, grp.tm, 512)
    tm_down = min(grp.tm, 512)
    return _mm_res(grp, act, w_down, x, l, which + 2, MACARON_WEIGHT, tm_down, 256, "ffn_down")


def kernel(x_prompt, x_sample, cache_k, cache_v, cache_idx_k, state_conv, state_ssm, state_ret, page_table,
           c_prompt, c_sample, w_in, conv_w, conv_b, ssm_a_log, ssm_dt_bias, ssm_d, ssm_norm, w_branch, w_out,
           ffn1_gate, ffn1_up, ffn1_down, ffn2_gate, ffn2_up, ffn2_down, norm_g, final_g, ada_w, ada_b):
    bp, seq, d = x_prompt.shape
    bd = x_sample.shape[0]
    inner = d
    conv_ch = inner + 2 * SSM_GROUPS * SSM_STATE
    n_ssm_heads = inner // SSM_HEAD_DIM
    vd = d // RET_HEADS
    kd = vd // 2
    past_len = page_table.shape[1] * cache_k.shape[2]

    mods = _ada(jnp.concatenate([c_sample, c_prompt], axis=0), ada_w, ada_b)
    mods = mods.reshape(DEPTH, bd + bp, N_MOD, d)
    mods_s = jnp.swapaxes(mods[:, :bd], 1, 2)
    mods_p = mods[:, bd:].reshape(DEPTH, bp * N_MOD, 1, d)
    gp = _Group(bp, seq, mods_p, min(1024, seq))
    gs = _Group(bd, 1, mods_s, bd)

    pos_p = jnp.arange(seq)
    pos_s = jnp.full((bd,), past_len)
    rot = ATT_HEAD_DIM // ROPE_FRACTION
    irot = IDX_DIM // ROPE_FRACTION
    tabs_att_p = _rope_tables(pos_p, rot, ATT_HEAD_DIM, ROPE_THETA)
    tabs_idx_p = _rope_tables(pos_p, irot, IDX_DIM, ROPE_THETA)
    tabs_att_s = _rope_tables(pos_s, rot, ATT_HEAD_DIM, ROPE_THETA)
    tabs_idx_s = _rope_tables(pos_s, irot, IDX_DIM, ROPE_THETA)
    tabs_ret_p = _ret_tables(pos_p, kd)
    tabs_ret_s = _ret_tables(pos_s, kd)
    lg, ret_consts = _ret_consts(SCAN_CHUNK)
    ret_consts["cd"] = jnp.broadcast_to(jnp.exp(SCAN_CHUNK * lg)[:, None, None], (RET_HEADS, 1, kd))
    gamma_rows = jnp.broadcast_to(jnp.repeat(jnp.exp(lg), vd)[None, :], (bd, d))

    xp = x_prompt.reshape(bp * seq, d)
    xs = x_sample.reshape(bd, d)
    conv0p = jnp.zeros((bp, 8, conv_ch), F32)
    ssm0 = jnp.zeros((bp, inner, SSM_STATE), F32)
    ret0 = jnp.zeros((bp, RET_HEADS, vd, kd), F32)

    outs_p = {k: [] for k in ("k", "v", "ik", "conv", "ssm", "ret")}
    outs_s = {k: [] for k in ("k", "v", "ik", "conv", "ssm", "ret")}
    tm_p = gp.tm
    tm_mid = min(512, seq)

    for l in range(DEPTH):
        wts = _pack_layer(l, w_in, conv_w, conv_b, ssm_a_log, ssm_dt_bias, ssm_d, ssm_norm, d)

        xp = _ffn(gp, xp, l, norm_g[l, 0], ffn1_gate[l], ffn1_up[l], ffn1_down[l], 0)
        h = _normmod(gp, xp, norm_g[l, 1], l, 3, 4)
        zx, dt, att, idx, ret, gl = _mixer_inputs(h, wts, tm_p)
        y_ssd, conv_new, ssm_new = _ssd_prompt(zx, dt, wts, conv0p, ssm0, bp, seq, inner)
        q_rot, k_rot, v_rows, iq_rot, ik_rot, iw = _rotary(att, idx, tabs_att_p, tabs_idx_p, min(256, seq), d)
        y_att = _dsa_prompt(q_rot, k_rot, v_rows, iq_rot, ik_rot, iw, bp, seq, d)
        y_ret, ret_new = _ret_prompt(ret, tabs_ret_p, ret_consts, ret0, bp, seq, d)
        merged = _merge([y_ssd, y_att, y_ret], w_branch[l], gl, tm_mid, 256)
        xp = _mm_res(gp, merged, w_out[l], xp, l, 5, 1.0, tm_p, 256, "out_proj")
        xp = _ffn(gp, xp, l, norm_g[l, 2], ffn2_gate[l], ffn2_up[l], ffn2_down[l], 6)
        outs_p["k"].append(k_rot.reshape(bp, seq, ATT_KV_HEADS, ATT_HEAD_DIM))
        outs_p["v"].append(v_rows.reshape(bp, seq, ATT_KV_HEADS, ATT_HEAD_DIM))
        outs_p["ik"].append(ik_rot.reshape(bp, seq, IDX_DIM))
        outs_p["conv"].append(conv_new)
        outs_p["ssm"].append(ssm_new.reshape(bp, n_ssm_heads, SSM_HEAD_DIM, SSM_STATE))
        outs_p["ret"].append(ret_new)

        xs = _ffn(gs, xs, l, norm_g[l, 0], ffn1_gate[l], ffn1_up[l], ffn1_down[l], 0)
        h = _normmod(gs, xs, norm_g[l, 1], l, 3, 4)
        zx, dt, att, idx, ret, gl = _mixer_inputs(h, wts, bd)
        cbuf = jnp.swapaxes(state_conv[:, l], 0, 1)
        conv_new, v_s, da_s, xs_act, bm, cm = _ssd_pre(zx, dt, cbuf, wts, inner)
        y_raw, ssm_new = _state_step(v_s, da_s, bm.reshape(bd, SSM_GROUPS, SSM_STATE),
                                     cm.reshape(bd, SSM_GROUPS, SSM_STATE),
                                     state_ssm[:, l].reshape(bd, inner, SSM_STATE))
        y_ssd = _ssd_post(y_raw, xs_act, zx, wts, inner)
        q_rot, k_rot, v_rows, iq_rot, ik_rot, iw = _rotary(att, idx, tabs_att_s, tabs_idx_s, bd, d)
        scores = _dsa_scores(page_table, iq_rot, iw, cache_idx_k, l)
        sel = _dsa_sample_select(scores, iq_rot, ik_rot, iw)
        y_att = _dsa_sample_attend(page_table, q_rot, cache_k, cache_v, sel, k_rot, v_rows, l, d)
        rq, rk = _ret_pre(ret, tabs_ret_s, kd)
        y_raw, ret_new = _state_step(ret[:, 2 * RET_HEADS * kd:2 * RET_HEADS * kd + d], gamma_rows,
                                     rk.reshape(bd, RET_HEADS, kd), rq.reshape(bd, RET_HEADS, kd),
                                     state_ret[:, l].reshape(bd, d, kd))
        y_ret = _ret_post(y_raw, ret, vd, 2 * RET_HEADS * kd + d)
        merged = _merge([y_ssd, y_att, y_ret], w_branch[l], gl, bd, 256)
        xs = _mm_res(gs, merged, w_out[l], xs, l, 5, 1.0, bd, 256, "out_proj")
        xs = _ffn(gs, xs, l, norm_g[l, 2], ffn2_gate[l], ffn2_up[l], ffn2_down[l], 6)
        outs_s["k"].append(k_rot.reshape(bd, 1, ATT_KV_HEADS, ATT_HEAD_DIM))
        outs_s["v"].append(v_rows.reshape(bd, 1, ATT_KV_HEADS, ATT_HEAD_DIM))
        outs_s["ik"].append(ik_rot.reshape(bd, 1, IDX_DIM))
        outs_s["conv"].append(jnp.swapaxes(conv_new, 0, 1))
        outs_s["ssm"].append(ssm_new.reshape(bd, n_ssm_heads, SSM_HEAD_DIM, SSM_STATE))
        outs_s["ret"].append(ret_new.reshape(bd, RET_HEADS, vd, kd))

    y_prompt = _final_norm(gp, xp, final_g).reshape(bp, seq, d)
    y_sample = _final_norm(gs, xs, final_g).reshape(bd, 1, d)
    st = lambda a: jnp.stack(a, axis=1)
    order = ("k", "v", "ik", "conv", "ssm", "ret")
    return (y_prompt, y_sample) + tuple(st(outs_p[k]) for k in order) + tuple(st(outs_s[k]) for k in order)
```

```python
import functools

import jax
import jax.numpy as jnp
from jax import lax
from jax.experimental import pallas as pl
from jax.experimental.pallas import tpu as pltpu

DEPTH = 2
SSM_HEAD_DIM = 64
SSM_GROUPS = 4
SSM_STATE = 128
CONV_WIDTH = 4
SCAN_CHUNK = 128
ATT_HEAD_DIM = 128
ATT_KV_HEADS = 4
IDX_HEADS = 8
IDX_DIM = 64
TOPK_MAX = 256
ROPE_THETA = 500000.0
ROPE_FRACTION = 4
RET_HEADS = 8
RET_THETA = 10000.0
MACARON_WEIGHT = 0.5
N_BRANCH = 3
N_MOD = 9
NORM_EPS = 1e-6

LANES = 128
VMEM_LIMIT_BYTES = 56 * 1024 * 1024

F32 = jnp.float32
BF16 = jnp.bfloat16


def _params(*sem):
    return pltpu.CompilerParams(dimension_semantics=sem, vmem_limit_bytes=VMEM_LIMIT_BYTES)


def _dot(a, b):
    return jnp.dot(a, b, preferred_element_type=F32)


def _dot_nt(a, b):
    return lax.dot_general(a, b, (((1,), (1,)), ((), ())), preferred_element_type=F32)


def _dot_tn(a, b):
    return lax.dot_general(a, b, (((0,), (0,)), ((), ())), preferred_element_type=F32)


def _split3(x):
    x1 = x.astype(BF16)
    r1 = x - x1.astype(F32)
    x2 = r1.astype(BF16)
    x3 = (r1 - x2.astype(F32)).astype(BF16)
    return x1, x2, x3


def _split2(x):
    x1 = x.astype(BF16)
    return x1, (x - x1.astype(F32)).astype(BF16)


def _sel_right(x, onehot):
    x1, x2, x3 = _split3(x)
    return _dot(x1, onehot) + _dot(x2, onehot) + _dot(x3, onehot)


def _sel_left(onehot, x):
    x1, x2, x3 = _split3(x)
    return _dot(onehot, x1) + _dot(onehot, x2) + _dot(onehot, x3)


def _silu(x):
    return x * jax.nn.sigmoid(x)


def _softplus(x):
    return jnp.maximum(x, 0.0) + jnp.log1p(jnp.exp(-jnp.abs(x)))


def _iota(shape, dim):
    return lax.broadcasted_iota(jnp.int32, shape, dim)


def _ada_kernel(c_ref, w_ref, b_ref, o_ref):
    a = _silu(c_ref[...]).astype(BF16)
    o_ref[...] = _dot(a, w_ref[...].astype(BF16)) + b_ref[...]


def _ada(c_all, ada_w, ada_b):
    depth, d, n = ada_w.shape
    rows = c_all.shape[0]
    tn = 1024
    return pl.pallas_call(
        _ada_kernel,
        grid=(depth, n // tn),
        in_specs=[pl.BlockSpec((rows, d), lambda l, j: (0, 0)),
                  pl.BlockSpec((None, d, tn), lambda l, j: (l, 0, j)),
                  pl.BlockSpec((None, 1, tn), lambda l, j: (l, 0, j))],
        out_specs=pl.BlockSpec((None, rows, tn), lambda l, j: (l, 0, j)),
        out_shape=jax.ShapeDtypeStruct((depth, rows, n), F32),
        compiler_params=_params("parallel", "parallel"),
        name="ada_mod",
    )(c_all, ada_w, ada_b.reshape(depth, 1, n))


class _Group:
    def __init__(self, batch, seq, mods, tm):
        self.batch, self.seq, self.m, self.tm = batch, seq, batch * seq, tm
        self.per_row = seq == 1
        self.mods = mods

    def mod(self, layer, which, tn, two_d):
        arr = self.mods[layer]
        if self.per_row:
            shape = (None, self.batch, tn)
            if two_d:
                return arr, pl.BlockSpec(shape, lambda i, j: (which, 0, j))
            return arr, pl.BlockSpec(shape, lambda i: (which, 0, 0))
        tm, seq = self.tm, self.seq
        shape = (None, 1, tn)
        if two_d:
            return arr, pl.BlockSpec(shape, lambda i, j: ((i * tm // seq) * N_MOD + which, 0, j))
        return arr, pl.BlockSpec(shape, lambda i: ((i * tm // seq) * N_MOD + which, 0, 0))


def _normmod_kernel(x_ref, g_ref, sc_ref, sh_ref, o_ref):
    x = x_ref[...]
    y = x * lax.rsqrt(jnp.mean(x * x, axis=-1, keepdims=True) + NORM_EPS) * g_ref[...]
    o_ref[...] = (y * (1.0 + sc_ref[...]) + sh_ref[...]).astype(o_ref.dtype)


def _normmod(grp, x, g, layer, w_shift, w_scale):
    d = x.shape[1]
    tm = min(256, grp.tm)
    sub = _Group(grp.batch, grp.seq, grp.mods, tm)
    sc_arr, sc_spec = sub.mod(layer, w_scale, d, False)
    sh_arr, sh_spec = sub.mod(layer, w_shift, d, False)
    return pl.pallas_call(
        _normmod_kernel,
        grid=(grp.m // tm,),
        in_specs=[pl.BlockSpec((tm, d), lambda i: (i, 0)),
                  pl.BlockSpec((1, d), lambda i: (0, 0)), sc_spec, sh_spec],
        out_specs=pl.BlockSpec((tm, d), lambda i: (i, 0)),
        out_shape=jax.ShapeDtypeStruct((grp.m, d), BF16),
        compiler_params=_params("parallel"),
        name="norm_mod",
    )(x, g.reshape(1, d), sc_arr, sh_arr)


def _final_norm_kernel(x_ref, g_ref, o_ref):
    x = x_ref[...]
    o_ref[...] = x * lax.rsqrt(jnp.mean(x * x, axis=-1, keepdims=True) + NORM_EPS) * g_ref[...]


def _final_norm(grp, x, g):
    d = x.shape[1]
    tm = min(256, grp.tm)
    return pl.pallas_call(
        _final_norm_kernel,
        grid=(grp.m // tm,),
        in_specs=[pl.BlockSpec((tm, d), lambda i: (i, 0)), pl.BlockSpec((1, d), lambda i: (0, 0))],
        out_specs=pl.BlockSpec((tm, d), lambda i: (i, 0)),
        out_shape=jax.ShapeDtypeStruct((grp.m, d), F32),
        compiler_params=_params("parallel"),
        name="final_norm",
    )(x, g.reshape(1, d))


def _mm_kernel(a_ref, w_ref, o_ref):
    o_ref[...] = _dot(a_ref[...], w_ref[...].astype(BF16)).astype(o_ref.dtype)


def _mm(a, w, tm, tn, out_dtype=F32, name="matmul"):
    m, k = a.shape
    n = w.shape[1]
    return pl.pallas_call(
        _mm_kernel,
        grid=(m // tm, n // tn),
        in_specs=[pl.BlockSpec((tm, k), lambda i, j: (i, 0)),
                  pl.BlockSpec((k, tn), lambda i, j: (0, j))],
        out_specs=pl.BlockSpec((tm, tn), lambda i, j: (i, j)),
        out_shape=jax.ShapeDtypeStruct((m, n), out_dtype),
        compiler_params=_params("parallel", "arbitrary"),
        name=name,
    )(a, w)


def _gateup_kernel(a_ref, wg_ref, wu_ref, o_ref):
    a = a_ref[...]
    g = _dot(a, wg_ref[...].astype(BF16))
    u = _dot(a, wu_ref[...].astype(BF16))
    o_ref[...] = (_silu(g) * u).astype(o_ref.dtype)


def _gateup(a, wg, wu, tm, tn):
    m, k = a.shape
    n = wg.shape[1]
    return pl.pallas_call(
        _gateup_kernel,
        grid=(m // tm, n // tn),
        in_specs=[pl.BlockSpec((tm, k), lambda i, j: (i, 0)),
                  pl.BlockSpec((k, tn), lambda i, j: (0, j)),
                  pl.BlockSpec((k, tn), lambda i, j: (0, j))],
        out_specs=pl.BlockSpec((tm, tn), lambda i, j: (i, j)),
        out_shape=jax.ShapeDtypeStruct((m, n), BF16),
        compiler_params=_params("parallel", "arbitrary"),
        name="ffn_gate_up",
    )(a, wg, wu)


def _mm_res_kernel(a_ref, w_ref, x_ref, g_ref, o_ref, acc_ref, *, coef):
    kk = pl.program_id(2)
    part = _dot(a_ref[...], w_ref[...].astype(BF16))

    @pl.when(kk == 0)
    def _():
        acc_ref[...] = part

    @pl.when(kk > 0)
    def _():
        acc_ref[...] += part

    @pl.when(kk == pl.num_programs(2) - 1)
    def _():
        o_ref[...] = x_ref[...] + coef * g_ref[...] * acc_ref[...]


def _mm_res(grp, a, w, x, layer, w_gate, coef, tm, tn, tk, name):
    m, k = a.shape
    n = w.shape[1]
    sub = _Group(grp.batch, grp.seq, grp.mods, tm)
    g_arr, g_spec2 = sub.mod(layer, w_gate, tn, True)
    g_spec = pl.BlockSpec(g_spec2.block_shape, lambda i, j, kk: g_spec2.index_map(i, j))
    return pl.pallas_call(
        functools.partial(_mm_res_kernel, coef=coef),
        grid=(m // tm, n // tn, k // tk),
        in_specs=[pl.BlockSpec((tm, tk), lambda i, j, kk: (i, kk)),
                  pl.BlockSpec((tk, tn), lambda i, j, kk: (kk, j)),
                  pl.BlockSpec((tm, tn), lambda i, j, kk: (i, j)),
                  g_spec],
        out_specs=pl.BlockSpec((tm, tn), lambda i, j, kk: (i, j)),
        out_shape=jax.ShapeDtypeStruct((m, n), F32),
        scratch_shapes=[pltpu.VMEM((tm, tn), F32)],
        compiler_params=_params("parallel", "parallel", "arbitrary"),
        name=name,
    )(a, w, x, g_arr)


def _merge_kernel(a0_ref, a1_ref, a2_ref, w_ref, g_ref, o_ref, acc_ref):
    br = pl.program_id(2)
    w = w_ref[...].astype(BF16)
    gate = jax.nn.sigmoid(g_ref[...])

    @pl.when(br == 0)
    def _():
        acc_ref[...] = gate * _dot(a0_ref[...], w)

    @pl.when(br == 1)
    def _():
        acc_ref[...] += gate * _dot(a1_ref[...], w)

    @pl.when(br == 2)
    def _():
        o_ref[...] = (acc_ref[...] + gate * _dot(a2_ref[...], w)).astype(o_ref.dtype)


def _merge(branches, w_branch_l, gate_logits, tm, tn):
    m, d = branches[0].shape
    nj = d // tn
    a_spec = pl.BlockSpec((tm, d), lambda i, j, br: (i, 0))
    return pl.pallas_call(
        _merge_kernel,
        grid=(m // tm, nj, N_BRANCH),
        in_specs=[a_spec, a_spec, a_spec,
                  pl.BlockSpec((None, d, tn), lambda i, j, br: (br, 0, j)),
                  pl.BlockSpec((tm, tn), lambda i, j, br: (i, br * nj + j))],
        out_specs=pl.BlockSpec((tm, tn), lambda i, j, br: (i, j)),
        out_shape=jax.ShapeDtypeStruct((m, d), BF16),
        scratch_shapes=[pltpu.VMEM((tm, tn), F32)],
        compiler_params=_params("parallel", "parallel", "arbitrary"),
        name="branch_merge",
    )(*branches, w_branch_l, gate_logits)


def _rotary_kernel(att_ref, idx_ref, ca_ref, sa_ref, sb_ref, ci_ref, sia_ref, sib_ref,
                   q_ref, k_ref, v_ref, iq_ref, ik_ref, iw_ref, kb_ref, vb_ref, iqb_ref, ikb_ref,
                   *, n_q, n_kv, n_iq):
    ca, sa, sb = ca_ref[...], sa_ref[...], sb_ref[...]
    ci, sia, sib = ci_ref[...], sia_ref[...], sib_ref[...]
    a_half = ATT_HEAD_DIM // ROPE_FRACTION // 2
    i_half = IDX_DIM // ROPE_FRACTION // 2

    def rot_a(x):
        return x * ca + pltpu.roll(x, LANES - a_half, 1) * sa + pltpu.roll(x, a_half, 1) * sb

    def rot_i(x):
        return x * ci + pltpu.roll(x, LANES - i_half, 1) * sia + pltpu.roll(x, i_half, 1) * sib

    for h in range(n_q):
        q_ref[:, h * LANES:(h + 1) * LANES] = rot_a(att_ref[:, h * LANES:(h + 1) * LANES]).astype(q_ref.dtype)
    for g in range(n_kv):
        kr = rot_a(att_ref[:, (n_q + g) * LANES:(n_q + g + 1) * LANES])
        k_ref[:, g * LANES:(g + 1) * LANES] = kr
        kb_ref[:, g * LANES:(g + 1) * LANES] = kr.astype(BF16)
    v = att_ref[:, (n_q + n_kv) * LANES:(n_q + 2 * n_kv) * LANES]
    v_ref[...] = v
    vb_ref[...] = v.astype(BF16)
    for j in range(n_iq):
        ir = rot_i(idx_ref[:, j * LANES:(j + 1) * LANES])
        iq_ref[:, j * LANES:(j + 1) * LANES] = ir
        iqb_ref[:, j * LANES:(j + 1) * LANES] = ir.astype(BF16)
    ikr = rot_i(idx_ref[:, n_iq * LANES:(n_iq + 1) * LANES])[:, :IDX_DIM]
    ik_ref[...] = ikr
    ikb_ref[...] = ikr.astype(BF16)
    iw_ref[...] = idx_ref[:, (n_iq + 1) * LANES:(n_iq + 2) * LANES] * (IDX_HEADS ** -0.5)


def _rotary(att, idx, tabs_a, tabs_i, tm, d_model):
    m = att.shape[0]
    p_rows = tabs_a[0].shape[0]
    n_q = d_model // ATT_HEAD_DIM
    n_kv = ATT_KV_HEADS
    n_iq = IDX_HEADS * IDX_DIM // LANES
    n_tab = p_rows // tm
    tab_spec = pl.BlockSpec((tm, LANES), lambda i: (i % n_tab, 0))
    kv_w = n_kv * ATT_HEAD_DIM
    return pl.pallas_call(
        functools.partial(_rotary_kernel, n_q=n_q, n_kv=n_kv, n_iq=n_iq),
        grid=(m // tm,),
        in_specs=[pl.BlockSpec((tm, att.shape[1]), lambda i: (i, 0)),
                  pl.BlockSpec((tm, idx.shape[1]), lambda i: (i, 0))] + [tab_spec] * 6,
        out_specs=[pl.BlockSpec((tm, d_model), lambda i: (i, 0)),
                   pl.BlockSpec((tm, kv_w), lambda i: (i, 0)),
                   pl.BlockSpec((tm, kv_w), lambda i: (i, 0)),
                   pl.BlockSpec((tm, n_iq * LANES), lambda i: (i, 0)),
                   pl.BlockSpec((tm, IDX_DIM), lambda i: (i, 0)),
                   pl.BlockSpec((tm, LANES), lambda i: (i, 0)),
                   pl.BlockSpec((tm, kv_w), lambda i: (i, 0)),
                   pl.BlockSpec((tm, kv_w), lambda i: (i, 0)),
                   pl.BlockSpec((tm, n_iq * LANES), lambda i: (i, 0)),
                   pl.BlockSpec((tm, IDX_DIM), lambda i: (i, 0))],
        out_shape=[jax.ShapeDtypeStruct((m, d_model), BF16),
                   jax.ShapeDtypeStruct((m, kv_w), F32),
                   jax.ShapeDtypeStruct((m, kv_w), F32),
                   jax.ShapeDtypeStruct((m, n_iq * LANES), F32),
                   jax.ShapeDtypeStruct((m, IDX_DIM), F32),
                   jax.ShapeDtypeStruct((m, LANES), F32),
                   jax.ShapeDtypeStruct((m, kv_w), BF16),
                   jax.ShapeDtypeStruct((m, kv_w), BF16),
                   jax.ShapeDtypeStruct((m, n_iq * LANES), BF16),
                   jax.ShapeDtypeStruct((m, IDX_DIM), BF16)],
        compiler_params=_params("parallel"),
        name="rotary",
    )(att, idx, *tabs_a, *tabs_i)


def _count_ge(s, t):
    return jnp.sum(jnp.where(s >= t, 1.0, 0.0), axis=-1, keepdims=True)


def _select_topk(score, adm, k):
    rows, keys = score.shape
    kf = float(k)
    s = jnp.where(adm, score, -jnp.inf)
    adm_f = jnp.where(adm, 1.0, 0.0)
    n_adm = jnp.sum(adm_f, axis=-1, keepdims=True)
    rmax = jnp.max(s, axis=-1, keepdims=True)
    rmin = jnp.min(jnp.where(adm, score, jnp.inf), axis=-1, keepdims=True)
    c_top = _count_ge(s, rmax)
    top_tie = c_top >= kf
    lo0 = jnp.where(top_tie, rmax, rmin)
    hi0 = jnp.where(top_tie, jnp.inf, rmax)
    c_lo0 = jnp.where(top_tie, c_top, n_adm)
    c_hi0 = jnp.where(top_tie, 0.0, c_top)
    done0 = jnp.where(top_tie | (n_adm <= kf), 1.0, 0.0)

    def cond(st):
        return jnp.min(st[4]) < 0.5

    def body(st):
        lo, hi, c_lo, c_hi, done = st
        mid = 0.5 * lo + 0.5 * hi
        c = _count_ge(s, mid)
        adjacent = (mid <= lo) | (mid >= hi)
        upd = (done < 0.5) & jnp.logical_not(adjacent)
        up_lo = upd & (c >= kf)
        up_hi = upd & (c < kf)
        lo = jnp.where(up_lo, mid, lo)
        c_lo = jnp.where(up_lo, c, c_lo)
        hi = jnp.where(up_hi, mid, hi)
        c_hi = jnp.where(up_hi, c, c_hi)
        done = jnp.where(adjacent | (c_lo == kf), 1.0, done)
        return lo, hi, c_lo, c_hi, done

    lo, hi, c_lo, c_hi, _ = lax.while_loop(cond, body, (lo0, hi0, c_lo0, c_hi0, done0))

    small = n_adm <= kf
    tied = jnp.logical_not(small) & (c_lo > kf)
    any_tied = jnp.max(jnp.where(tied, 1.0, 0.0)) > 0.5

    def plain():
        return jnp.where(s >= lo, 1.0, 0.0)

    def ranked():
        sure = s >= hi
        band = jnp.where((s >= lo) & jnp.logical_not(sure), 1.0, 0.0)
        quota = kf - c_hi
        tri = jnp.where(_iota((LANES, LANES), 0) <= _iota((LANES, LANES), 1), 1.0, 0.0).astype(BF16)
        off = jnp.zeros((rows, 1), F32)
        pieces = []
        for blk in range(keys // LANES):
            bb = band[:, blk * LANES:(blk + 1) * LANES]
            rank = _dot(bb.astype(BF16), tri) + off
            pieces.append(jnp.where((bb > 0.5) & (rank <= quota), 1.0, 0.0))
            off = off + jnp.sum(bb, axis=-1, keepdims=True)
        keep = jnp.concatenate(pieces, axis=-1)
        return jnp.where(sure, 1.0, keep)

    sel = lax.cond(any_tied, ranked, plain)
    return jnp.where(small, adm_f, sel)


def _dsa_prompt_body(qi, q_ref, k_ref, v_ref, iq_ref, ik_ref, iw_ref, o_ref, *, ext, k_top, n_heads):
    tq = q_ref.shape[0]
    q_per_kv = n_heads // ATT_KV_HEADS
    ik = ik_ref[0:ext, :]
    iw = iw_ref[...]
    score = jnp.zeros((tq, ext), F32)
    for h in range(IDX_HEADS):
        s = _dot_nt(iq_ref[:, h * IDX_DIM:(h + 1) * IDX_DIM], ik)
        score = score + jnp.maximum(s * (IDX_DIM ** -0.5), 0.0) * iw[:, h:h + 1]

    q_pos = qi * tq + _iota((tq, ext), 0)
    adm = _iota((tq, ext), 1) <= q_pos
    sel = _select_topk(score, adm, k_top) > 0.5

    for g in range(ATT_KV_HEADS):
        kg = k_ref[0:ext, g * ATT_HEAD_DIM:(g + 1) * ATT_HEAD_DIM]
        vg = v_ref[0:ext, g * ATT_HEAD_DIM:(g + 1) * ATT_HEAD_DIM]
        heads = [g * q_per_kv + hh for hh in range(q_per_kv)]
        q_stack = jnp.concatenate([q_ref[:, h * ATT_HEAD_DIM:(h + 1) * ATT_HEAD_DIM] for h in heads], axis=0)
        logits = _dot_nt(q_stack, kg) * (ATT_HEAD_DIM ** -0.5)
        ps, dens = [], []
        for hh in range(q_per_kv):
            lg = jnp.where(sel, logits[hh * tq:(hh + 1) * tq], -jnp.inf)
            p = jnp.exp(lg - jnp.max(lg, axis=-1, keepdims=True))
            dens.append(jnp.sum(p, axis=-1, keepdims=True))
            ps.append(p.astype(BF16))
        o_stack = _dot(jnp.concatenate(ps, axis=0), vg)
        for hh, h in enumerate(heads):
            o = o_stack[hh * tq:(hh + 1) * tq] / dens[hh]
            o_ref[:, h * ATT_HEAD_DIM:(h + 1) * ATT_HEAD_DIM] = o.astype(o_ref.dtype)


def _dsa_prompt_kernel(q_ref, k_ref, v_ref, iq_ref, ik_ref, iw_ref, o_ref, *, k_top, n_heads, unit):
    qi = pl.program_id(1)
    tq = q_ref.shape[0]
    seq = k_ref.shape[0]
    n_units = ((qi + 1) * tq + unit - 1) // unit
    for c in range(1, seq // unit + 1):
        @pl.when(n_units == c)
        def _(c=c):
            _dsa_prompt_body(qi, q_ref, k_ref, v_ref, iq_ref, ik_ref, iw_ref, o_ref,
                             ext=c * unit, k_top=k_top, n_heads=n_heads)


def _dsa_prompt(q, k, v, iq, ik, iw, batch, seq, d_model):
    tq = 128
    nq = seq // tq
    unit = min(512, seq)
    k_top = min(TOPK_MAX, seq // 4)
    kv_w = ATT_KV_HEADS * ATT_HEAD_DIM
    return pl.pallas_call(
        functools.partial(_dsa_prompt_kernel, k_top=k_top, n_heads=d_model // ATT_HEAD_DIM, unit=unit),
        grid=(batch, nq),
        in_specs=[pl.BlockSpec((tq, d_model), lambda b, i: (b * nq + i, 0)),
                  pl.BlockSpec((seq, kv_w), lambda b, i: (b, 0)),
                  pl.BlockSpec((seq, kv_w), lambda b, i: (b, 0)),
                  pl.BlockSpec((tq, IDX_HEADS * IDX_DIM), lambda b, i: (b * nq + i, 0)),
                  pl.BlockSpec((seq, IDX_DIM), lambda b, i: (b, 0)),
                  pl.BlockSpec((tq, LANES), lambda b, i: (b * nq + i, 0))],
        out_specs=pl.BlockSpec((tq, d_model), lambda b, i: (b * nq + i, 0)),
        out_shape=jax.ShapeDtypeStruct((batch * seq, d_model), BF16),
        compiler_params=_params("parallel", "arbitrary"),
        name="dsa_prompt",
    )(q, k, v, iq, ik, iw)


def _dsa_score_kernel(pt_ref, iq_ref, iw_ref, *rest):
    kpages, o_ref = rest[:-1], rest[-1]
    a = iq_ref[...].astype(BF16)
    w = iw_ref[...]
    for t, kpage_ref in enumerate(kpages):
        s = _dot_nt(a, kpage_ref[...].astype(BF16))
        s = jnp.maximum(s * (IDX_DIM ** -0.5), 0.0) * w
        o_ref[t:t + 1, :] = jnp.sum(s, axis=0, keepdims=True)


def _dsa_scores(page_table, iq, iw, cache_idx_k, layer):
    bd, n_pages = page_table.shape
    page = cache_idx_k.shape[2]
    pg = min(16, n_pages)
    page_specs = [pl.BlockSpec((None, None, page, IDX_DIM),
                               functools.partial(lambda b, s, pt, t: (pt[b, s * pg + t], layer, 0, 0), t=t))
                  for t in range(pg)]
    grid_spec = pltpu.PrefetchScalarGridSpec(
        num_scalar_prefetch=1,
        grid=(bd, n_pages // pg),
        in_specs=[pl.BlockSpec((None, IDX_HEADS, IDX_DIM), lambda b, s, pt: (b, 0, 0)),
                  pl.BlockSpec((None, IDX_HEADS, 1), lambda b, s, pt: (b, 0, 0))] + page_specs,
        out_specs=pl.BlockSpec((None, None, pg, page), lambda b, s, pt: (b, s, 0, 0)),
    )
    out = pl.pallas_call(
        _dsa_score_kernel,
        grid_spec=grid_spec,
        out_shape=jax.ShapeDtypeStruct((bd, n_pages // pg, pg, page), F32),
        compiler_params=_params("parallel", "arbitrary"),
        name="dsa_sample_scores",
    )(page_table, iq.reshape(bd, IDX_HEADS, IDX_DIM), iw[:, :IDX_HEADS].reshape(bd, IDX_HEADS, 1),
      *([cache_idx_k] * pg))
    return out.reshape(bd, n_pages * page)


def _dsa_sample_select_kernel(sc_ref, iq_ref, ik_ref, iw_ref, o_ref, *, k_top):
    bd, past = sc_ref.shape
    iw = iw_ref[...]
    ik = ik_ref[...]
    s_new = jnp.zeros((bd, 1), F32)
    for h in range(IDX_HEADS):
        d = jnp.sum(iq_ref[:, h * IDX_DIM:(h + 1) * IDX_DIM] * ik, axis=-1, keepdims=True)
        s_new = s_new + jnp.maximum(d * (IDX_DIM ** -0.5), 0.0) * iw[:, h:h + 1]
    lane = _iota((bd, LANES), 1)
    tail = jnp.where(lane == 0, s_new, 0.0)
    score = jnp.concatenate([sc_ref[...], tail], axis=-1)
    adm = _iota((bd, past + LANES), 1) <= past
    o_ref[...] = _select_topk(score, adm, k_top)


def _dsa_sample_select(scores, iq, ik, iw):
    bd, past = scores.shape
    k_top = min(TOPK_MAX, (past + 1) // 4)
    return pl.pallas_call(
        functools.partial(_dsa_sample_select_kernel, k_top=k_top),
        out_shape=jax.ShapeDtypeStruct((bd, past + LANES), F32),
        compiler_params=pltpu.CompilerParams(vmem_limit_bytes=VMEM_LIMIT_BYTES),
        name="dsa_sample_select",
    )(scores, iq, ik, iw)


def _dsa_sample_attend_kernel(pt_ref, q_ref, *rest, pg):
    kpages, vpages = rest[:pg], rest[pg:2 * pg]
    sel_ref, seltail_ref, knew_ref, vnew_ref, o_ref, m_ref, l_ref, acc_ref = rest[2 * pg:]
    p = pl.program_id(1)
    n_pages = pl.num_programs(1) - 1
    n_heads = q_ref.shape[0]
    q_per_kv = n_heads // ATT_KV_HEADS
    scale = ATT_HEAD_DIM ** -0.5
    head_group = _iota((n_heads, LANES), 0) // q_per_kv

    @pl.when(p == 0)
    def _():
        m_ref[...] = jnp.full(m_ref.shape, -1e30, F32)
        l_ref[...] = jnp.zeros(l_ref.shape, F32)
        acc_ref[...] = jnp.zeros(acc_ref.shape, F32)

    def update(logits, sel, pv_fn):
        logits = jnp.where(sel, logits, -1e30)
        m_old = m_ref[...]
        m_new = jnp.maximum(m_old, jnp.max(logits, axis=-1, keepdims=True))
        alpha = jnp.exp(m_old - m_new)
        pr = jnp.where(sel, jnp.exp(logits - m_new), 0.0)
        l_ref[...] = alpha * l_ref[...] + jnp.sum(pr, axis=-1, keepdims=True)
        acc_ref[...] = alpha * acc_ref[...] + pv_fn(pr)
        m_ref[...] = m_new

    @pl.when(p < n_pages)
    def _():
        q = q_ref[...]
        parts, sels = [], []
        for t in range(pg):
            lt = jnp.zeros((n_heads, LANES), F32)
            for g in range(ATT_KV_HEADS):
                lt = jnp.where(head_group == g, _dot_nt(q, kpages[t][:, g, :].astype(BF16)), lt)
            parts.append(lt)
            sels.append(sel_ref[t:t + 1, :])
        logits = jnp.concatenate(parts, axis=1) * scale
        sel = jnp.broadcast_to(jnp.concatenate(sels, axis=1) > 0.5, logits.shape)

        def pv(pr):
            out = jnp.zeros((n_heads, ATT_HEAD_DIM), F32)
            for g in range(ATT_KV_HEADS):
                og = jnp.zeros((n_heads, ATT_HEAD_DIM), F32)
                for t in range(pg):
                    og = og + _dot(pr[:, t * LANES:(t + 1) * LANES].astype(BF16), vpages[t][:, g, :].astype(BF16))
                out = jnp.where(head_group == g, og, out)
            return out

        update(logits, sel, pv)

    @pl.when(p == n_pages)
    def _():
        qf = q_ref[...].astype(F32)
        kexp = jnp.zeros((n_heads, ATT_HEAD_DIM), F32)
        vexp = jnp.zeros((n_heads, ATT_HEAD_DIM), F32)
        for g in range(ATT_KV_HEADS):
            kexp = jnp.where(head_group == g, knew_ref[:, g * ATT_HEAD_DIM:(g + 1) * ATT_HEAD_DIM], kexp)
            vexp = jnp.where(head_group == g, vnew_ref[:, g * ATT_HEAD_DIM:(g + 1) * ATT_HEAD_DIM], vexp)
        lg = jnp.sum(qf * kexp, axis=-1, keepdims=True) * scale
        sel_new = seltail_ref[:, 0:1] > 0.5
        lane0 = _iota((n_heads, LANES), 1) == 0
        logits = jnp.broadcast_to(lg, (n_heads, LANES))
        sel = lane0 & jnp.broadcast_to(sel_new, (n_heads, LANES))
        update(logits, sel, lambda pr: jnp.sum(pr, axis=-1, keepdims=True) * vexp)
        o_ref[...] = (acc_ref[...] / l_ref[...]).astype(o_ref.dtype)


def _dsa_sample_attend(page_table, q, cache_k, cache_v, sel, k_new, v_new, layer, d_model):
    bd, n_pages = page_table.shape
    page = cache_k.shape[2]
    n_heads = d_model // ATT_HEAD_DIM
    kv_w = ATT_KV_HEADS * ATT_HEAD_DIM
    pg = min(8, n_pages)
    n_steps = n_pages // pg
    last = n_steps - 1
    page_specs = [pl.BlockSpec((None, None, page, ATT_KV_HEADS, ATT_HEAD_DIM),
                               functools.partial(
                                   lambda b, s, pt, t: (pt[b, jnp.minimum(s, last) * pg + t], layer, 0, 0, 0), t=t))
                  for t in range(pg)]
    grid_spec = pltpu.PrefetchScalarGridSpec(
        num_scalar_prefetch=1,
        grid=(bd, n_steps + 1),
        in_specs=[pl.BlockSpec((None, n_heads, ATT_HEAD_DIM), lambda b, s, pt: (b, 0, 0))]
        + page_specs + page_specs
        + [pl.BlockSpec((None, None, pg, page), lambda b, s, pt: (b, jnp.minimum(s, last), 0, 0)),
           pl.BlockSpec((None, 1, page), lambda b, s, pt: (b, 0, 0)),
           pl.BlockSpec((None, 1, kv_w), lambda b, s, pt: (b, 0, 0)),
           pl.BlockSpec((None, 1, kv_w), lambda b, s, pt: (b, 0, 0))],
        out_specs=pl.BlockSpec((None, n_heads, ATT_HEAD_DIM), lambda b, s, pt: (b, 0, 0)),
        scratch_shapes=[pltpu.VMEM((n_heads, 1), F32), pltpu.VMEM((n_heads, 1), F32),
                        pltpu.VMEM((n_heads, ATT_HEAD_DIM), F32)],
    )
    past = n_pages * page
    out = pl.pallas_call(
        functools.partial(_dsa_sample_attend_kernel, pg=pg),
        grid_spec=grid_spec,
        out_shape=jax.ShapeDtypeStruct((bd, n_heads, ATT_HEAD_DIM), BF16),
        compiler_params=_params("parallel", "arbitrary"),
        name="dsa_sample_attend",
    )(page_table, q.reshape(bd, n_heads, ATT_HEAD_DIM), *([cache_k] * pg), *([cache_v] * pg),
      sel[:, :past].reshape(bd, n_steps, pg, page), sel[:, past:].reshape(bd, 1, page),
      k_new.reshape(bd, 1, kv_w), v_new.reshape(bd, 1, kv_w))
    return out.reshape(bd, d_model)


def _ssd_kernel(z_ref, xs_ref, bc_ref, dt_ref, cw_ref, cb_ref, alog_ref, dtb_ref, dfull_ref, ng_ref,
                e_ref, et_ref, conv0_ref, s0_ref, y_ref, convn_ref, sn_ref, ext_ref, st_ref, yb_ref):
    c = pl.program_id(1)
    nc = pl.num_programs(1)
    q = xs_ref.shape[0]
    inner = xs_ref.shape[1]
    gw = inner // SSM_GROUPS
    heads_pg = gw // SSM_HEAD_DIM
    n_heads = inner // SSM_HEAD_DIM
    pad = 8
    tail = CONV_WIDTH - 1

    @pl.when(c == 0)
    def _():
        ext_ref[0:pad, :] = conv0_ref[...]
        st_ref[...] = s0_ref[...]

    ext_ref[pad:pad + q, 0:inner] = xs_ref[...]
    ext_ref[pad:pad + q, inner:] = bc_ref[...]
    conv = cb_ref[...] + cw_ref[0:1, :] * ext_ref[pl.ds(pad - 3, q), :]
    for i in range(1, CONV_WIDTH):
        conv = conv + cw_ref[i:i + 1, :] * ext_ref[pl.ds(pad - 3 + i, q), :]

    @pl.when(c == nc - 1)
    def _():
        convn_ref[...] = ext_ref[pl.ds(pad + q - tail, tail), :]

    ext_ref[0:pad, :] = ext_ref[q:q + pad, :]
    act = _silu(conv)
    xs = act[:, :inner]
    bm = act[:, inner:inner + SSM_GROUPS * SSM_STATE].astype(BF16)
    cm = act[:, inner + SSM_GROUPS * SSM_STATE:].astype(BF16)

    dt = _softplus(dt_ref[...] + dtb_ref[...])
    la = dt * (-jnp.exp(alog_ref[...]))
    tril = jnp.where(_iota((q, q), 1) <= _iota((q, q), 0), 1.0, 0.0).astype(BF16)
    acs = _sel_left(tril, la)
    acs_t = acs.T
    e = e_ref[...]
    acs_full = _sel_right(acs, e)
    dt_full = _sel_right(dt, e)
    exp_acs = jnp.exp(acs_full)
    to_end = jnp.exp(acs_full[q - 1:q, :] - acs_full)
    v = xs * dt_full
    vb = v.astype(BF16)
    vte = (v * to_end).astype(BF16)
    causal = _iota((q, q), 1) <= _iota((q, q), 0)
    lane_lo = _iota((q, LANES), 1) < SSM_HEAD_DIM
    zero_b = jnp.zeros((q, LANES), BF16)

    a_last = jnp.broadcast_to(acs_t[:, q - 1:q], (LANES, LANES))
    decay_rows = jnp.exp(_sel_left(et_ref[...], a_last))

    for g in range(SSM_GROUPS):
        bg = bm[:, g * SSM_STATE:(g + 1) * SSM_STATE]
        cg = cm[:, g * SSM_STATE:(g + 1) * SSM_STATE]
        qk = _dot_nt(cg, bg)
        rows = slice(g * gw, (g + 1) * gw)
        s_old = st_ref[rows, :]
        y_off = _dot_nt(cg, s_old.astype(BF16))
        new_state = _dot_tn(vte[:, rows], bg)
        st_ref[rows, :] = s_old * decay_rows[rows, :] + new_state
        yb_ref[:, rows] = y_off * exp_acs[:, rows]
        for pr in range(heads_pg // 2):
            h0 = g * heads_pg + 2 * pr
            ms = []
            for h in (h0, h0 + 1):
                seg = acs[:, h:h + 1] - acs_t[h:h + 1, :]
                ms.append((qk * jnp.exp(jnp.where(causal, seg, -jnp.inf))).astype(BF16))
            m2 = jnp.concatenate(ms, axis=1)
            cols = slice(h0 * SSM_HEAD_DIM, (h0 + 2) * SSM_HEAD_DIM)
            vp = vb[:, cols]
            v2 = jnp.concatenate([jnp.where(lane_lo, vp, zero_b), jnp.where(lane_lo, zero_b, vp)], axis=0)
            yb_ref[:, cols] += _dot(m2, v2)

    y = (yb_ref[...] + xs * dfull_ref[...]) * _silu(z_ref[...])
    for g in range(SSM_GROUPS):
        yg = y[:, g * gw:(g + 1) * gw]
        yn = yg * lax.rsqrt(jnp.mean(yg * yg, axis=-1, keepdims=True) + NORM_EPS)
        y_ref[:, g * gw:(g + 1) * gw] = (yn * ng_ref[:, g * gw:(g + 1) * gw]).astype(y_ref.dtype)

    @pl.when(c == nc - 1)
    def _():
        sn_ref[...] = st_ref[...]


def _ssd_prompt(zx, dt, wts, conv0p, s0, batch, seq, inner):
    q = SCAN_CHUNK
    nc = seq // q
    conv_ch = inner + 2 * SSM_GROUPS * SSM_STATE
    bc_w = conv_ch - inner
    full = lambda shape: pl.BlockSpec(shape, lambda b, c: (0,) * len(shape))
    return pl.pallas_call(
        _ssd_kernel,
        grid=(batch, nc),
        in_specs=[pl.BlockSpec((q, inner), lambda b, c: (b * nc + c, 0)),
                  pl.BlockSpec((q, inner), lambda b, c: (b * nc + c, 1)),
                  pl.BlockSpec((q, bc_w), lambda b, c: (b * nc + c, 2 * inner // bc_w)),
                  pl.BlockSpec((q, LANES), lambda b, c: (b * nc + c, 0)),
                  full((CONV_WIDTH, conv_ch)), full((1, conv_ch)), full((1, LANES)), full((1, LANES)),
                  full((1, inner)), full((1, inner)), full((LANES, inner)), full((inner, LANES)),
                  pl.BlockSpec((None, 8, conv_ch), lambda b, c: (b, 0, 0)),
                  pl.BlockSpec((None, inner, SSM_STATE), lambda b, c: (b, 0, 0))],
        out_specs=[pl.BlockSpec((q, inner), lambda b, c: (b * nc + c, 0)),
                   pl.BlockSpec((None, CONV_WIDTH - 1, conv_ch), lambda b, c: (b, 0, 0)),
                   pl.BlockSpec((None, inner, SSM_STATE), lambda b, c: (b, 0, 0))],
        out_shape=[jax.ShapeDtypeStruct((batch * seq, inner), BF16),
                   jax.ShapeDtypeStruct((batch, CONV_WIDTH - 1, conv_ch), F32),
                   jax.ShapeDtypeStruct((batch, inner, SSM_STATE), F32)],
        scratch_shapes=[pltpu.VMEM((q + 8, conv_ch), F32), pltpu.VMEM((inner, SSM_STATE), F32),
                        pltpu.VMEM((q, inner), F32)],
        compiler_params=_params("parallel", "arbitrary"),
        name="ssd_prompt",
    )(zx, zx, zx, dt, wts["conv_w"], wts["conv_b"], wts["a_log"], wts["dt_bias"], wts["d_full"],
      wts["ssm_norm"], wts["e"], wts["et"], conv0p, s0)


def _ret_kernel(rq_ref, rk_ref, rv_ref, rg_ref, cos_ref, sin_ref, dmat_ref, rin_ref, tend_ref, cd_ref,
                s0_ref, y_ref, sn_ref, st_ref):
    c = pl.program_id(1)
    nc = pl.num_programs(1)
    kd = st_ref.shape[2]
    vd = st_ref.shape[1]

    @pl.when(c == 0)
    def _():
        st_ref[...] = s0_ref[...]

    cos, sin = cos_ref[...], sin_ref[...]

    def rot(x):
        return x * cos + pltpu.roll(x, kd // 2, 1) * sin

    for h in range(RET_HEADS):
        qr = rot(rq_ref[:, h * kd:(h + 1) * kd]).astype(BF16)
        kr = (rot(rk_ref[:, h * kd:(h + 1) * kd]) * (kd ** -0.5)).astype(BF16)
        vh = rv_ref[:, h * vd:(h + 1) * vd]
        m = (_dot_nt(qr, kr) * dmat_ref[h]).astype(BF16)
        s_old = st_ref[h]
        y = _dot(m, vh.astype(BF16)) + _dot_nt(qr, s_old.astype(BF16)) * rin_ref[h]
        st_ref[h] = s_old * cd_ref[h] + _dot_tn((vh * tend_ref[h]).astype(BF16), kr)
        yn = y * lax.rsqrt(jnp.mean(y * y, axis=-1, keepdims=True) + NORM_EPS)
        y_ref[:, h * vd:(h + 1) * vd] = (_silu(rg_ref[:, h * vd:(h + 1) * vd]) * yn).astype(y_ref.dtype)

    @pl.when(c == nc - 1)
    def _():
        sn_ref[...] = st_ref[...]


def _ret_prompt(ret, tabs, consts, s0, batch, seq, d_model):
    q = SCAN_CHUNK
    nc = seq // q
    vd = d_model // RET_HEADS
    kd = vd // 2
    qk_w = RET_HEADS * kd
    full = lambda shape: pl.BlockSpec(shape, lambda b, c: (0,) * len(shape))
    return pl.pallas_call(
        _ret_kernel,
        grid=(batch, nc),
        in_specs=[pl.BlockSpec((q, qk_w), lambda b, c: (b * nc + c, 0)),
                  pl.BlockSpec((q, qk_w), lambda b, c: (b * nc + c, 1)),
                  pl.BlockSpec((q, d_model), lambda b, c: (b * nc + c, 1)),
                  pl.BlockSpec((q, d_model), lambda b, c: (b * nc + c, 2)),
                  pl.BlockSpec((q, kd), lambda b, c: (c, 0)),
                  pl.BlockSpec((q, kd), lambda b, c: (c, 0)),
                  full((RET_HEADS, q, q)), full((RET_HEADS, q, 1)), full((RET_HEADS, q, 1)),
                  full((RET_HEADS, 1, kd)),
                  pl.BlockSpec((None, RET_HEADS, vd, kd), lambda b, c: (b, 0, 0, 0))],
        out_specs=[pl.BlockSpec((q, d_model), lambda b, c: (b * nc + c, 0)),
                   pl.BlockSpec((None, RET_HEADS, vd, kd), lambda b, c: (b, 0, 0, 0))],
        out_shape=[jax.ShapeDtypeStruct((batch * seq, d_model), BF16),
                   jax.ShapeDtypeStruct((batch, RET_HEADS, vd, kd), F32)],
        scratch_shapes=[pltpu.VMEM((RET_HEADS, vd, kd), F32)],
        compiler_params=_params("parallel", "arbitrary"),
        name="ret_prompt",
    )(ret, ret, ret, ret, tabs[0], tabs[1], consts["dmat"], consts["rin"], consts["tend"], consts["cd"], s0)


def _ssd_pre_kernel(zx_ref, dt_ref, cbuf_ref, cw_ref, cb_ref, alog_ref, dtb_ref, e_ref,
                    convn_ref, v_ref, da_ref, xs_ref, bm_ref, cm_ref, *, inner):
    x = zx_ref[:, inner:]
    conv = cb_ref[...] + cw_ref[CONV_WIDTH - 1:CONV_WIDTH, :] * x
    for i in range(CONV_WIDTH - 1):
        conv = conv + cw_ref[i:i + 1, :] * cbuf_ref[i]
    for i in range(CONV_WIDTH - 2):
        convn_ref[i] = cbuf_ref[i + 1]
    convn_ref[CONV_WIDTH - 2] = x
    act = _silu(conv)
    xs = act[:, :inner]
    n_bc = SSM_GROUPS * SSM_STATE
    dt = _softplus(dt_ref[...] + dtb_ref[...])
    la = dt * (-jnp.exp(alog_ref[...]))
    e = e_ref[...]
    v_ref[...] = xs * _sel_right(dt, e)
    da_ref[...] = jnp.exp(_sel_right(la, e))
    xs_ref[...] = xs
    bm_ref[...] = act[:, inner:inner + n_bc]
    cm_ref[...] = act[:, inner + n_bc:]


def _ssd_pre(zx, dt, cbuf, wts, inner):
    bd = zx.shape[0]
    conv_ch = inner + 2 * SSM_GROUPS * SSM_STATE
    n_bc = SSM_GROUPS * SSM_STATE
    sd = jax.ShapeDtypeStruct
    return pl.pallas_call(
        functools.partial(_ssd_pre_kernel, inner=inner),
        out_shape=[sd((CONV_WIDTH - 1, bd, conv_ch), F32), sd((bd, inner), F32), sd((bd, inner), F32),
                   sd((bd, inner), F32), sd((bd, n_bc), F32), sd((bd, n_bc), F32)],
        compiler_params=pltpu.CompilerParams(vmem_limit_bytes=VMEM_LIMIT_BYTES),
        name="ssd_step_pre",
    )(zx, dt, cbuf, wts["conv_w"], wts["conv_b"], wts["a_log"], wts["dt_bias"], wts["e"])


def _state_step_kernel(v_ref, d_ref, k_ref, q_ref, s_ref, y_ref, sn_ref, vt_ref, dt_ref, pad_ref):
    b = pl.program_id(0)
    bd, rows = v_ref.shape
    n_groups = k_ref.shape[0]
    gr = rows // n_groups

    @pl.when(b == 0)
    def _():
        pad_ref[...] = jnp.zeros(pad_ref.shape, F32)
        for src, dst in ((v_ref, vt_ref), (d_ref, dt_ref)):
            pad_ref[0:bd, :] = src[...]
            for t in range(rows // LANES):
                dst[t * LANES:(t + 1) * LANES, :] = pad_ref[:, t * LANES:(t + 1) * LANES].T

    onehot = jnp.where(_iota((LANES, LANES), 0) == b, 1.0, 0.0).astype(BF16)
    v_rows = _sel_right(vt_ref[...], onehot)
    d_rows = _sel_right(dt_ref[...], onehot)
    for g in range(n_groups):
        rs = slice(g * gr, (g + 1) * gr)
        s_new = s_ref[rs, :] * d_rows[rs, :] + v_rows[rs, :] * k_ref[g:g + 1, :]
        sn_ref[rs, :] = s_new
        q8 = jnp.broadcast_to(q_ref[g:g + 1, :], (8, q_ref.shape[1])).astype(BF16)
        y_ref[pl.ds(b, 1), rs] = _dot_nt(q8, s_new.astype(BF16))[0:1, :]


def _state_step(v, d, k, q, state):
    bd, rows = v.shape
    n_groups, kd = k.shape[1], k.shape[2]
    return pl.pallas_call(
        _state_step_kernel,
        grid=(bd,),
        in_specs=[pl.BlockSpec((bd, rows), lambda b: (0, 0)),
                  pl.BlockSpec((bd, rows), lambda b: (0, 0)),
                  pl.BlockSpec((None, n_groups, kd), lambda b: (b, 0, 0)),
                  pl.BlockSpec((None, n_groups, kd), lambda b: (b, 0, 0)),
                  pl.BlockSpec((None, rows, kd), lambda b: (b, 0, 0))],
        out_specs=[pl.BlockSpec((bd, rows), lambda b: (0, 0)),
                   pl.BlockSpec((None, rows, kd), lambda b: (b, 0, 0))],
        out_shape=[jax.ShapeDtypeStruct((bd, rows), F32),
                   jax.ShapeDtypeStruct((bd, rows, kd), F32)],
        scratch_shapes=[pltpu.VMEM((rows, LANES), F32), pltpu.VMEM((rows, LANES), F32),
                        pltpu.VMEM((LANES, rows), F32)],
        compiler_params=_params("arbitrary"),
        name="state_step",
    )(v, d, k, q, state)


def _ssd_post_kernel(y_ref, xs_ref, zx_ref, dfull_ref, ng_ref, o_ref, *, inner):
    y = (y_ref[...] + xs_ref[...] * dfull_ref[...]) * _silu(zx_ref[:, :inner])
    gw = inner // SSM_GROUPS
    for g in range(SSM_GROUPS):
        yg = y[:, g * gw:(g + 1) * gw]
        yn = yg * lax.rsqrt(jnp.mean(yg * yg, axis=-1, keepdims=True) + NORM_EPS)
        o_ref[:, g * gw:(g + 1) * gw] = (yn * ng_ref[:, g * gw:(g + 1) * gw]).astype(o_ref.dtype)


def _ssd_post(y, xs, zx, wts, inner):
    return pl.pallas_call(
        functools.partial(_ssd_post_kernel, inner=inner),
        out_shape=jax.ShapeDtypeStruct(y.shape, BF16),
        compiler_params=pltpu.CompilerParams(vmem_limit_bytes=VMEM_LIMIT_BYTES),
        name="ssd_step_post",
    )(y, xs, zx, wts["d_full"], wts["ssm_norm"])


def _ret_pre_kernel(ret_ref, cos_ref, sin_ref, q_ref, k_ref, *, kd):
    cos, sin = cos_ref[...], sin_ref[...]
    qk_w = RET_HEADS * kd

    def rot(x):
        return x * cos + pltpu.roll(x, kd // 2, 1) * sin

    for h in range(RET_HEADS):
        q_ref[:, h * kd:(h + 1) * kd] = rot(ret_ref[:, h * kd:(h + 1) * kd])
        k_ref[:, h * kd:(h + 1) * kd] = rot(ret_ref[:, qk_w + h * kd:qk_w + (h + 1) * kd]) * (kd ** -0.5)


def _ret_pre(ret, tabs, kd):
    bd = ret.shape[0]
    sd = jax.ShapeDtypeStruct
    return pl.pallas_call(
        functools.partial(_ret_pre_kernel, kd=kd),
        out_shape=[sd((bd, RET_HEADS * kd), F32), sd((bd, RET_HEADS * kd), F32)],
        compiler_params=pltpu.CompilerParams(vmem_limit_bytes=VMEM_LIMIT_BYTES),
        name="ret_step_pre",
    )(ret, tabs[0], tabs[1])


def _ret_post_kernel(y_ref, ret_ref, o_ref, *, vd, g_off):
    for h in range(RET_HEADS):
        y = y_ref[:, h * vd:(h + 1) * vd]
        yn = y * lax.rsqrt(jnp.mean(y * y, axis=-1, keepdims=True) + NORM_EPS)
        g = ret_ref[:, g_off + h * vd:g_off + (h + 1) * vd]
        o_ref[:, h * vd:(h + 1) * vd] = (_silu(g) * yn).astype(o_ref.dtype)


def _ret_post(y, ret, vd, g_off):
    return pl.pallas_call(
        functools.partial(_ret_post_kernel, vd=vd, g_off=g_off),
        out_shape=jax.ShapeDtypeStruct(y.shape, BF16),
        compiler_params=pltpu.CompilerParams(vmem_limit_bytes=VMEM_LIMIT_BYTES),
        name="ret_step_post",
    )(y, ret)


def _rope_tables(pos, rot_dim, width, theta):
    half = rot_dim // 2
    inv_freq = theta ** (-jnp.arange(half, dtype=F32) * (2.0 / rot_dim))
    ang = pos.astype(F32)[:, None] * inv_freq[None, :]
    cos, sin = jnp.cos(ang), jnp.sin(ang)
    n = pos.shape[0]
    ones = jnp.ones((n, width - rot_dim), F32)
    zeros = jnp.zeros((n, width - rot_dim), F32)
    z_half = jnp.zeros((n, half), F32)
    c = jnp.concatenate([cos, cos, ones], axis=1)
    sa = jnp.concatenate([-sin, z_half, zeros], axis=1)
    sb = jnp.concatenate([z_half, sin, zeros], axis=1)
    rep = LANES // width
    return tuple(jnp.tile(t, (1, rep)) for t in (c, sa, sb))


def _ret_tables(pos, kd):
    half = kd // 2
    inv_freq = RET_THETA ** (-jnp.arange(half, dtype=F32) * (2.0 / kd))
    ang = pos.astype(F32)[:, None] * inv_freq[None, :]
    cos, sin = jnp.cos(ang), jnp.sin(ang)
    return jnp.concatenate([cos, cos], axis=1), jnp.concatenate([-sin, sin], axis=1)


def _ret_consts(q):
    lg = jnp.log(1.0 - 2.0 ** (-5.0 - jnp.arange(RET_HEADS, dtype=F32)))
    i = jnp.arange(q, dtype=F32)
    diff = i[:, None] - i[None, :]
    dmat = jnp.where(diff >= 0, jnp.exp(diff[None] * lg[:, None, None]), 0.0)
    rin = jnp.exp((i + 1.0)[None, :] * lg[:, None])[..., None]
    tend = jnp.exp((q - 1.0 - i)[None, :] * lg[:, None])[..., None]
    return lg, {"dmat": dmat, "rin": rin, "tend": tend}


def _pack_layer(l, w_in, conv_w, conv_b, ssm_a_log, ssm_dt_bias, ssm_d, ssm_norm, d_model):
    inner = d_model
    conv_ch = inner + 2 * SSM_GROUPS * SSM_STATE
    n_ssm_heads = inner // SSM_HEAD_DIM
    n_att = d_model
    kv_w = ATT_KV_HEADS * ATT_HEAD_DIM
    iq_w = IDX_HEADS * IDX_DIM
    vd = d_model // RET_HEADS
    kd = vd // 2
    widths = (inner, conv_ch, n_ssm_heads, n_att, kv_w, kv_w, iq_w, IDX_DIM, IDX_HEADS,
              RET_HEADS * kd, RET_HEADS * kd, RET_HEADS * vd, RET_HEADS * vd, N_BRANCH * d_model)
    offs = [0]
    for w in widths:
        offs.append(offs[-1] + w)
    w = w_in[l]
    col = lambda a, b: w[:, offs[a]:offs[b]]
    zpad = lambda n: jnp.zeros((w.shape[0], n), w.dtype)
    pad1 = lambda x: jnp.concatenate([x, jnp.zeros((LANES - x.shape[0],), x.dtype)]).reshape(1, LANES)
    head_of = jnp.arange(inner) // SSM_HEAD_DIM
    e = (jnp.arange(LANES)[:, None] == head_of[None, :]).astype(BF16)
    return {
        "w_ssd": col(0, 2),
        "w_dt": jnp.concatenate([col(2, 3), zpad(LANES - n_ssm_heads)], axis=1),
        "w_att": col(3, 6),
        "w_idx": jnp.concatenate([col(6, 8), zpad(LANES - IDX_DIM), col(8, 9), zpad(LANES - IDX_HEADS)], axis=1),
        "w_ret": col(9, 13),
        "w_gate": col(13, 14),
        "conv_w": conv_w[l], "conv_b": conv_b[l].reshape(1, conv_ch),
        "a_log": pad1(ssm_a_log[l]), "dt_bias": pad1(ssm_dt_bias[l]),
        "d_full": jnp.repeat(ssm_d[l], SSM_HEAD_DIM).reshape(1, inner),
        "ssm_norm": ssm_norm[l].reshape(1, inner),
        "e": e, "et": e.T,
    }


def _mixer_inputs(h, wts, tm):
    zx = _mm(h, wts["w_ssd"], tm, 1024, name="in_proj_ssd")
    dt = _mm(h, wts["w_dt"], tm, LANES, name="in_proj_dt")
    att = _mm(h, wts["w_att"], tm, 1024, name="in_proj_att")
    idx = _mm(h, wts["w_idx"], tm, wts["w_idx"].shape[1], name="in_proj_idx")
    ret = _mm(h, wts["w_ret"], tm, 1024, name="in_proj_ret")
    gl = _mm(h, wts["w_gate"], tm, 1024, name="in_proj_gate")
    return zx, dt, att, idx, ret, gl


def _ffn(grp, x, l, norm_g_l, w_gate, w_up, w_down, which):
    d_ff = w_gate.shape[1]
    h = _normmod(grp, x, norm_g_l, l, which, which + 1)
    act = _gateup(h, w_gate, w_up, grp.tm, 512)
    return _mm_res(grp, act, w_down, x, l, which + 2, MACARON_WEIGHT, grp.tm, 512, d_ff // 4, "ffn_down")


def kernel(x_prompt, x_sample, cache_k, cache_v, cache_idx_k, state_conv, state_ssm, state_ret, page_table,
           c_prompt, c_sample, w_in, conv_w, conv_b, ssm_a_log, ssm_dt_bias, ssm_d, ssm_norm, w_branch, w_out,
           ffn1_gate, ffn1_up, ffn1_down, ffn2_gate, ffn2_up, ffn2_down, norm_g, final_g, ada_w, ada_b):
    bp, seq, d = x_prompt.shape
    bd = x_sample.shape[0]
    inner = d
    conv_ch = inner + 2 * SSM_GROUPS * SSM_STATE
    n_ssm_heads = inner // SSM_HEAD_DIM
    vd = d // RET_HEADS
    kd = vd // 2
    past_len = page_table.shape[1] * cache_k.shape[2]

    mods = _ada(jnp.concatenate([c_sample, c_prompt], axis=0), ada_w, ada_b)
    mods = mods.reshape(DEPTH, bd + bp, N_MOD, d)
    mods_s = jnp.swapaxes(mods[:, :bd], 1, 2)
    mods_p = mods[:, bd:].reshape(DEPTH, bp * N_MOD, 1, d)
    gp = _Group(bp, seq, mods_p, min(1024, seq))
    gs = _Group(bd, 1, mods_s, bd)

    pos_p = jnp.arange(seq)
    pos_s = jnp.full((bd,), past_len)
    rot = ATT_HEAD_DIM // ROPE_FRACTION
    irot = IDX_DIM // ROPE_FRACTION
    tabs_att_p = _rope_tables(pos_p, rot, ATT_HEAD_DIM, ROPE_THETA)
    tabs_idx_p = _rope_tables(pos_p, irot, IDX_DIM, ROPE_THETA)
    tabs_att_s = _rope_tables(pos_s, rot, ATT_HEAD_DIM, ROPE_THETA)
    tabs_idx_s = _rope_tables(pos_s, irot, IDX_DIM, ROPE_THETA)
    tabs_ret_p = _ret_tables(pos_p, kd)
    tabs_ret_s = _ret_tables(pos_s, kd)
    lg, ret_consts = _ret_consts(SCAN_CHUNK)
    ret_consts["cd"] = jnp.broadcast_to(jnp.exp(SCAN_CHUNK * lg)[:, None, None], (RET_HEADS, 1, kd))
    gamma_rows = jnp.broadcast_to(jnp.repeat(jnp.exp(lg), vd)[None, :], (bd, d))

    xp = x_prompt.reshape(bp * seq, d)
    xs = x_sample.reshape(bd, d)
    conv0p = jnp.zeros((bp, 8, conv_ch), F32)
    ssm0 = jnp.zeros((bp, inner, SSM_STATE), F32)
    ret0 = jnp.zeros((bp, RET_HEADS, vd, kd), F32)

    outs_p = {k: [] for k in ("k", "v", "ik", "conv", "ssm", "ret")}
    outs_s = {k: [] for k in ("k", "v", "ik", "conv", "ssm", "ret")}
    tm_p = gp.tm

    for l in range(DEPTH):
        wts = _pack_layer(l, w_in, conv_w, conv_b, ssm_a_log, ssm_dt_bias, ssm_d, ssm_norm, d)

        xp = _ffn(gp, xp, l, norm_g[l, 0], ffn1_gate[l], ffn1_up[l], ffn1_down[l], 0)
        h = _normmod(gp, xp, norm_g[l, 1], l, 3, 4)
        zx, dt, att, idx, ret, gl = _mixer_inputs(h, wts, tm_p)
        y_ssd, conv_new, ssm_new = _ssd_prompt(zx, dt, wts, conv0p, ssm0, bp, seq, inner)
        q_rot, k_rot, v_rows, _, ik_rot, iw, kb, vb, iqb, ikb = _rotary(
            att, idx, tabs_att_p, tabs_idx_p, min(256, seq), d)
        y_att = _dsa_prompt(q_rot, kb, vb, iqb, ikb, iw, bp, seq, d)
        y_ret, ret_new = _ret_prompt(ret, tabs_ret_p, ret_consts, ret0, bp, seq, d)
        merged = _merge([y_ssd, y_att, y_ret], w_branch[l], gl, tm_p, 256)
        xp = _mm_res(gp, merged, w_out[l], xp, l, 5, 1.0, tm_p, 512, d, "out_proj")
        xp = _ffn(gp, xp, l, norm_g[l, 2], ffn2_gate[l], ffn2_up[l], ffn2_down[l], 6)
        outs_p["k"].append(k_rot.reshape(bp, seq, ATT_KV_HEADS, ATT_HEAD_DIM))
        outs_p["v"].append(v_rows.reshape(bp, seq, ATT_KV_HEADS, ATT_HEAD_DIM))
        outs_p["ik"].append(ik_rot.reshape(bp, seq, IDX_DIM))
        outs_p["conv"].append(conv_new)
        outs_p["ssm"].append(ssm_new.reshape(bp, n_ssm_heads, SSM_HEAD_DIM, SSM_STATE))
        outs_p["ret"].append(ret_new)

        xs = _ffn(gs, xs, l, norm_g[l, 0], ffn1_gate[l], ffn1_up[l], ffn1_down[l], 0)
        h = _normmod(gs, xs, norm_g[l, 1], l, 3, 4)
        zx, dt, att, idx, ret, gl = _mixer_inputs(h, wts, bd)
        cbuf = jnp.swapaxes(state_conv[:, l], 0, 1)
        conv_new, v_s, da_s, xs_act, bm, cm = _ssd_pre(zx, dt, cbuf, wts, inner)
        y_raw, ssm_new = _state_step(v_s, da_s, bm.reshape(bd, SSM_GROUPS, SSM_STATE),
                                     cm.reshape(bd, SSM_GROUPS, SSM_STATE),
                                     state_ssm[:, l].reshape(bd, inner, SSM_STATE))
        y_ssd = _ssd_post(y_raw, xs_act, zx, wts, inner)
        q_rot, k_rot, v_rows, iq_rot, ik_rot, iw = _rotary(att, idx, tabs_att_s, tabs_idx_s, bd, d)[:6]
        scores = _dsa_scores(page_table, iq_rot, iw, cache_idx_k, l)
        sel = _dsa_sample_select(scores, iq_rot, ik_rot, iw)
        y_att = _dsa_sample_attend(page_table, q_rot, cache_k, cache_v, sel, k_rot, v_rows, l, d)
        rq, rk = _ret_pre(ret, tabs_ret_s, kd)
        y_raw, ret_new = _state_step(ret[:, 2 * RET_HEADS * kd:2 * RET_HEADS * kd + d], gamma_rows,
                                     rk.reshape(bd, RET_HEADS, kd), rq.reshape(bd, RET_HEADS, kd),
                                     state_ret[:, l].reshape(bd, d, kd))
        y_ret = _ret_post(y_raw, ret, vd, 2 * RET_HEADS * kd + d)
        merged = _merge([y_ssd, y_att, y_ret], w_branch[l], gl, bd, 256)
        xs = _mm_res(gs, merged, w_out[l], xs, l, 5, 1.0, bd, 512, d, "out_proj")
        xs = _ffn(gs, xs, l, norm_g[l, 2], ffn2_gate[l], ffn2_up[l], ffn2_down[l], 6)
        outs_s["k"].append(k_rot.reshape(bd, 1, ATT_KV_HEADS, ATT_HEAD_DIM))
        outs_s["v"].append(v_rows.reshape(bd, 1, ATT_KV_HEADS, ATT_HEAD_DIM))
        outs_s["ik"].append(ik_rot.reshape(bd, 1, IDX_DIM))
        outs_s["conv"].append(jnp.swapaxes(conv_new, 0, 1))
        outs_s["ssm"].append(ssm_new.reshape(bd, n_ssm_heads, SSM_HEAD_DIM, SSM_STATE))
        outs_s["ret"].append(ret_new.reshape(bd, RET_HEADS, vd, kd))

    y_prompt = _final_norm(gp, xp, final_g).reshape(bp, seq, d)
    y_sample = _final_norm(gs, xs, final_g).reshape(bd, 1, d)
    st = lambda a: jnp.stack(a, axis=1)
    order = ("k", "v", "ik", "conv", "ssm", "ret")
    return (y_prompt, y_sample) + tuple(st(outs_p[k]) for k in order) + tuple(st(outs_s[k]) for k in order)
```

```python
import functools

import jax
import jax.numpy as jnp
from jax import lax
from jax.experimental import pallas as pl
from jax.experimental.pallas import tpu as pltpu

DEPTH = 2
SSM_HEAD_DIM = 64
SSM_GROUPS = 4
SSM_STATE = 128
CONV_WIDTH = 4
SCAN_CHUNK = 128
ATT_HEAD_DIM = 128
ATT_KV_HEADS = 4
IDX_HEADS = 8
IDX_DIM = 64
TOPK_MAX = 256
ROPE_THETA = 500000.0
ROPE_FRACTION = 4
RET_HEADS = 8
RET_THETA = 10000.0
MACARON_WEIGHT = 0.5
N_BRANCH = 3
N_MOD = 9
NORM_EPS = 1e-6

LANES = 128
VMEM_LIMIT_BYTES = 56 * 1024 * 1024

F32 = jnp.float32
BF16 = jnp.bfloat16


def _params(*sem):
    return pltpu.CompilerParams(dimension_semantics=sem, vmem_limit_bytes=VMEM_LIMIT_BYTES)


def _dot(a, b):
    return jnp.dot(a, b, preferred_element_type=F32)


def _dot_nt(a, b):
    return lax.dot_general(a, b, (((1,), (1,)), ((), ())), preferred_element_type=F32)


def _dot_tn(a, b):
    return lax.dot_general(a, b, (((0,), (0,)), ((), ())), preferred_element_type=F32)


def _split3(x):
    x1 = x.astype(BF16)
    r1 = x - x1.astype(F32)
    x2 = r1.astype(BF16)
    x3 = (r1 - x2.astype(F32)).astype(BF16)
    return x1, x2, x3


def _split2(x):
    x1 = x.astype(BF16)
    return x1, (x - x1.astype(F32)).astype(BF16)


def _sel_right(x, onehot):
    x1, x2, x3 = _split3(x)
    return _dot(x1, onehot) + _dot(x2, onehot) + _dot(x3, onehot)


def _sel_left(onehot, x):
    x1, x2, x3 = _split3(x)
    return _dot(onehot, x1) + _dot(onehot, x2) + _dot(onehot, x3)


def _silu(x):
    return x * jax.nn.sigmoid(x)


def _softplus(x):
    return jnp.maximum(x, 0.0) + jnp.log1p(jnp.exp(-jnp.abs(x)))


def _iota(shape, dim):
    return lax.broadcasted_iota(jnp.int32, shape, dim)


def _ada_kernel(c_ref, w_ref, b_ref, o_ref):
    a = _silu(c_ref[...]).astype(BF16)
    o_ref[...] = _dot(a, w_ref[...].astype(BF16)) + b_ref[...]


def _ada(c_all, ada_w, ada_b):
    depth, d, n = ada_w.shape
    rows = c_all.shape[0]
    tn = 1024
    return pl.pallas_call(
        _ada_kernel,
        grid=(depth, n // tn),
        in_specs=[pl.BlockSpec((rows, d), lambda l, j: (0, 0)),
                  pl.BlockSpec((None, d, tn), lambda l, j: (l, 0, j)),
                  pl.BlockSpec((None, 1, tn), lambda l, j: (l, 0, j))],
        out_specs=pl.BlockSpec((None, rows, tn), lambda l, j: (l, 0, j)),
        out_shape=jax.ShapeDtypeStruct((depth, rows, n), F32),
        compiler_params=_params("parallel", "parallel"),
        name="ada_mod",
    )(c_all, ada_w, ada_b.reshape(depth, 1, n))


class _Group:
    def __init__(self, batch, seq, mods, tm):
        self.batch, self.seq, self.m, self.tm = batch, seq, batch * seq, tm
        self.per_row = seq == 1
        self.mods = mods

    def mod(self, layer, which, tn, two_d):
        arr = self.mods[layer]
        if self.per_row:
            shape = (None, self.batch, tn)
            if two_d:
                return arr, pl.BlockSpec(shape, lambda i, j: (which, 0, j))
            return arr, pl.BlockSpec(shape, lambda i: (which, 0, 0))
        tm, seq = self.tm, self.seq
        shape = (None, 1, tn)
        if two_d:
            return arr, pl.BlockSpec(shape, lambda i, j: ((i * tm // seq) * N_MOD + which, 0, j))
        return arr, pl.BlockSpec(shape, lambda i: ((i * tm // seq) * N_MOD + which, 0, 0))


def _normmod_kernel(x_ref, g_ref, sc_ref, sh_ref, o_ref):
    x = x_ref[...]
    y = x * lax.rsqrt(jnp.mean(x * x, axis=-1, keepdims=True) + NORM_EPS) * g_ref[...]
    o_ref[...] = (y * (1.0 + sc_ref[...]) + sh_ref[...]).astype(o_ref.dtype)


def _normmod(grp, x, g, layer, w_shift, w_scale):
    d = x.shape[1]
    tm = min(256, grp.tm)
    sub = _Group(grp.batch, grp.seq, grp.mods, tm)
    sc_arr, sc_spec = sub.mod(layer, w_scale, d, False)
    sh_arr, sh_spec = sub.mod(layer, w_shift, d, False)
    return pl.pallas_call(
        _normmod_kernel,
        grid=(grp.m // tm,),
        in_specs=[pl.BlockSpec((tm, d), lambda i: (i, 0)),
                  pl.BlockSpec((1, d), lambda i: (0, 0)), sc_spec, sh_spec],
        out_specs=pl.BlockSpec((tm, d), lambda i: (i, 0)),
        out_shape=jax.ShapeDtypeStruct((grp.m, d), BF16),
        compiler_params=_params("parallel"),
        name="norm_mod",
    )(x, g.reshape(1, d), sc_arr, sh_arr)


def _final_norm_kernel(x_ref, g_ref, o_ref):
    x = x_ref[...]
    o_ref[...] = x * lax.rsqrt(jnp.mean(x * x, axis=-1, keepdims=True) + NORM_EPS) * g_ref[...]


def _final_norm(grp, x, g):
    d = x.shape[1]
    tm = min(256, grp.tm)
    return pl.pallas_call(
        _final_norm_kernel,
        grid=(grp.m // tm,),
        in_specs=[pl.BlockSpec((tm, d), lambda i: (i, 0)), pl.BlockSpec((1, d), lambda i: (0, 0))],
        out_specs=pl.BlockSpec((tm, d), lambda i: (i, 0)),
        out_shape=jax.ShapeDtypeStruct((grp.m, d), F32),
        compiler_params=_params("parallel"),
        name="final_norm",
    )(x, g.reshape(1, d))


def _mm_kernel(a_ref, w_ref, o_ref):
    o_ref[...] = _dot(a_ref[...], w_ref[...].astype(BF16)).astype(o_ref.dtype)


def _mm(a, w, tm, tn, out_dtype=F32, name="matmul"):
    m, k = a.shape
    n = w.shape[1]
    return pl.pallas_call(
        _mm_kernel,
        grid=(m // tm, n // tn),
        in_specs=[pl.BlockSpec((tm, k), lambda i, j: (i, 0)),
                  pl.BlockSpec((k, tn), lambda i, j: (0, j))],
        out_specs=pl.BlockSpec((tm, tn), lambda i, j: (i, j)),
        out_shape=jax.ShapeDtypeStruct((m, n), out_dtype),
        compiler_params=_params("parallel", "arbitrary"),
        name=name,
    )(a, w)


def _gateup_kernel(a_ref, wg_ref, wu_ref, o_ref):
    a = a_ref[...]
    g = _dot(a, wg_ref[...].astype(BF16))
    u = _dot(a, wu_ref[...].astype(BF16))
    o_ref[...] = (_silu(g) * u).astype(o_ref.dtype)


def _gateup(a, wg, wu, tm, tn):
    m, k = a.shape
    n = wg.shape[1]
    return pl.pallas_call(
        _gateup_kernel,
        grid=(m // tm, n // tn),
        in_specs=[pl.BlockSpec((tm, k), lambda i, j: (i, 0)),
                  pl.BlockSpec((k, tn), lambda i, j: (0, j)),
                  pl.BlockSpec((k, tn), lambda i, j: (0, j))],
        out_specs=pl.BlockSpec((tm, tn), lambda i, j: (i, j)),
        out_shape=jax.ShapeDtypeStruct((m, n), BF16),
        compiler_params=_params("parallel", "arbitrary"),
        name="ffn_gate_up",
    )(a, wg, wu)


def _mm_res_kernel(a_ref, w_ref, x_ref, g_ref, o_ref, acc_ref, *, coef):
    kk = pl.program_id(2)
    part = _dot(a_ref[...], w_ref[...].astype(BF16))

    @pl.when(kk == 0)
    def _():
        acc_ref[...] = part

    @pl.when(kk > 0)
    def _():
        acc_ref[...] += part

    @pl.when(kk == pl.num_programs(2) - 1)
    def _():
        o_ref[...] = x_ref[...] + coef * g_ref[...] * acc_ref[...]


def _mm_res(grp, a, w, x, layer, w_gate, coef, tm, tn, tk, name):
    m, k = a.shape
    n = w.shape[1]
    sub = _Group(grp.batch, grp.seq, grp.mods, tm)
    g_arr, g_spec2 = sub.mod(layer, w_gate, tn, True)
    g_spec = pl.BlockSpec(g_spec2.block_shape, lambda i, j, kk: g_spec2.index_map(i, j))
    return pl.pallas_call(
        functools.partial(_mm_res_kernel, coef=coef),
        grid=(m // tm, n // tn, k // tk),
        in_specs=[pl.BlockSpec((tm, tk), lambda i, j, kk: (i, kk)),
                  pl.BlockSpec((tk, tn), lambda i, j, kk: (kk, j)),
                  pl.BlockSpec((tm, tn), lambda i, j, kk: (i, j)),
                  g_spec],
        out_specs=pl.BlockSpec((tm, tn), lambda i, j, kk: (i, j)),
        out_shape=jax.ShapeDtypeStruct((m, n), F32),
        scratch_shapes=[pltpu.VMEM((tm, tn), F32)],
        compiler_params=_params("parallel", "parallel", "arbitrary"),
        name=name,
    )(a, w, x, g_arr)


def _merge_kernel(a0_ref, a1_ref, a2_ref, w_ref, g_ref, o_ref, acc_ref):
    br = pl.program_id(2)
    w = w_ref[...].astype(BF16)
    gate = jax.nn.sigmoid(g_ref[...])

    @pl.when(br == 0)
    def _():
        acc_ref[...] = gate * _dot(a0_ref[...], w)

    @pl.when(br == 1)
    def _():
        acc_ref[...] += gate * _dot(a1_ref[...], w)

    @pl.when(br == 2)
    def _():
        o_ref[...] = (acc_ref[...] + gate * _dot(a2_ref[...], w)).astype(o_ref.dtype)


def _merge(branches, w_branch_l, gate_logits, tm, tn):
    m, d = branches[0].shape
    nj = d // tn
    a_spec = pl.BlockSpec((tm, d), lambda i, j, br: (i, 0))
    return pl.pallas_call(
        _merge_kernel,
        grid=(m // tm, nj, N_BRANCH),
        in_specs=[a_spec, a_spec, a_spec,
                  pl.BlockSpec((None, d, tn), lambda i, j, br: (br, 0, j)),
                  pl.BlockSpec((tm, tn), lambda i, j, br: (i, br * nj + j))],
        out_specs=pl.BlockSpec((tm, tn), lambda i, j, br: (i, j)),
        out_shape=jax.ShapeDtypeStruct((m, d), BF16),
        scratch_shapes=[pltpu.VMEM((tm, tn), F32)],
        compiler_params=_params("parallel", "parallel", "arbitrary"),
        name="branch_merge",
    )(*branches, w_branch_l, gate_logits)


def _rotary_kernel(att_ref, idx_ref, ca_ref, sa_ref, sb_ref, ci_ref, sia_ref, sib_ref,
                   q_ref, k_ref, v_ref, iq_ref, ik_ref, iw_ref, kb_ref, vb_ref, iqb_ref, ikb_ref,
                   *, n_q, n_kv, n_iq):
    ca, sa, sb = ca_ref[...], sa_ref[...], sb_ref[...]
    ci, sia, sib = ci_ref[...], sia_ref[...], sib_ref[...]
    a_half = ATT_HEAD_DIM // ROPE_FRACTION // 2
    i_half = IDX_DIM // ROPE_FRACTION // 2

    def rot_a(x):
        return x * ca + pltpu.roll(x, LANES - a_half, 1) * sa + pltpu.roll(x, a_half, 1) * sb

    def rot_i(x):
        return x * ci + pltpu.roll(x, LANES - i_half, 1) * sia + pltpu.roll(x, i_half, 1) * sib

    for h in range(n_q):
        q_ref[:, h * LANES:(h + 1) * LANES] = rot_a(att_ref[:, h * LANES:(h + 1) * LANES]).astype(q_ref.dtype)
    for g in range(n_kv):
        kr = rot_a(att_ref[:, (n_q + g) * LANES:(n_q + g + 1) * LANES])
        k_ref[:, g * LANES:(g + 1) * LANES] = kr
        kb_ref[:, g * LANES:(g + 1) * LANES] = kr.astype(BF16)
    v = att_ref[:, (n_q + n_kv) * LANES:(n_q + 2 * n_kv) * LANES]
    v_ref[...] = v
    vb_ref[...] = v.astype(BF16)
    for j in range(n_iq):
        ir = rot_i(idx_ref[:, j * LANES:(j + 1) * LANES])
        iq_ref[:, j * LANES:(j + 1) * LANES] = ir
        iqb_ref[:, j * LANES:(j + 1) * LANES] = ir.astype(BF16)
    ikr = rot_i(idx_ref[:, n_iq * LANES:(n_iq + 1) * LANES])[:, :IDX_DIM]
    ik_ref[...] = ikr
    ikb_ref[...] = ikr.astype(BF16)
    iw_ref[...] = idx_ref[:, (n_iq + 1) * LANES:(n_iq + 2) * LANES] * (IDX_HEADS ** -0.5)


def _rotary(att, idx, tabs_a, tabs_i, tm, d_model):
    m = att.shape[0]
    p_rows = tabs_a[0].shape[0]
    n_q = d_model // ATT_HEAD_DIM
    n_kv = ATT_KV_HEADS
    n_iq = IDX_HEADS * IDX_DIM // LANES
    n_tab = p_rows // tm
    tab_spec = pl.BlockSpec((tm, LANES), lambda i: (i % n_tab, 0))
    kv_w = n_kv * ATT_HEAD_DIM
    return pl.pallas_call(
        functools.partial(_rotary_kernel, n_q=n_q, n_kv=n_kv, n_iq=n_iq),
        grid=(m // tm,),
        in_specs=[pl.BlockSpec((tm, att.shape[1]), lambda i: (i, 0)),
                  pl.BlockSpec((tm, idx.shape[1]), lambda i: (i, 0))] + [tab_spec] * 6,
        out_specs=[pl.BlockSpec((tm, d_model), lambda i: (i, 0)),
                   pl.BlockSpec((tm, kv_w), lambda i: (i, 0)),
                   pl.BlockSpec((tm, kv_w), lambda i: (i, 0)),
                   pl.BlockSpec((tm, n_iq * LANES), lambda i: (i, 0)),
                   pl.BlockSpec((tm, IDX_DIM), lambda i: (i, 0)),
                   pl.BlockSpec((tm, LANES), lambda i: (i, 0)),
                   pl.BlockSpec((tm, kv_w), lambda i: (i, 0)),
                   pl.BlockSpec((tm, kv_w), lambda i: (i, 0)),
                   pl.BlockSpec((tm, n_iq * LANES), lambda i: (i, 0)),
                   pl.BlockSpec((tm, IDX_DIM), lambda i: (i, 0))],
        out_shape=[jax.ShapeDtypeStruct((m, d_model), BF16),
                   jax.ShapeDtypeStruct((m, kv_w), F32),
                   jax.ShapeDtypeStruct((m, kv_w), F32),
                   jax.ShapeDtypeStruct((m, n_iq * LANES), F32),
                   jax.ShapeDtypeStruct((m, IDX_DIM), F32),
                   jax.ShapeDtypeStruct((m, LANES), F32),
                   jax.ShapeDtypeStruct((m, kv_w), BF16),
                   jax.ShapeDtypeStruct((m, kv_w), BF16),
                   jax.ShapeDtypeStruct((m, n_iq * LANES), BF16),
                   jax.ShapeDtypeStruct((m, IDX_DIM), BF16)],
        compiler_params=_params("parallel"),
        name="rotary",
    )(att, idx, *tabs_a, *tabs_i)


def _count_ge(s, t):
    return jnp.sum(jnp.where(s >= t, 1.0, 0.0), axis=-1, keepdims=True)


def _select_topk(score, adm, k):
    rows, keys = score.shape
    kf = float(k)
    s = jnp.where(adm, score, -jnp.inf)
    adm_f = jnp.where(adm, 1.0, 0.0)
    n_adm = jnp.sum(adm_f, axis=-1, keepdims=True)
    rmax = jnp.max(s, axis=-1, keepdims=True)
    rmin = jnp.min(jnp.where(adm, score, jnp.inf), axis=-1, keepdims=True)
    c_top = _count_ge(s, rmax)
    top_tie = c_top >= kf
    lo0 = jnp.where(top_tie, rmax, rmin)
    hi0 = jnp.where(top_tie, jnp.inf, rmax)
    c_lo0 = jnp.where(top_tie, c_top, n_adm)
    c_hi0 = jnp.where(top_tie, 0.0, c_top)
    done0 = jnp.where(top_tie | (n_adm <= kf), 1.0, 0.0)

    def cond(st):
        return jnp.min(st[4]) < 0.5

    def body(st):
        lo, hi, c_lo, c_hi, done = st
        mid = 0.5 * lo + 0.5 * hi
        c = _count_ge(s, mid)
        adjacent = (mid <= lo) | (mid >= hi)
        upd = (done < 0.5) & jnp.logical_not(adjacent)
        up_lo = upd & (c >= kf)
        up_hi = upd & (c < kf)
        lo = jnp.where(up_lo, mid, lo)
        c_lo = jnp.where(up_lo, c, c_lo)
        hi = jnp.where(up_hi, mid, hi)
        c_hi = jnp.where(up_hi, c, c_hi)
        done = jnp.where(adjacent | (c_lo == kf), 1.0, done)
        return lo, hi, c_lo, c_hi, done

    lo, hi, c_lo, c_hi, _ = lax.while_loop(cond, body, (lo0, hi0, c_lo0, c_hi0, done0))

    small = n_adm <= kf
    tied = jnp.logical_not(small) & (c_lo > kf)
    any_tied = jnp.max(jnp.where(tied, 1.0, 0.0)) > 0.5

    def plain():
        return jnp.where(s >= lo, 1.0, 0.0)

    def ranked():
        sure = s >= hi
        band = jnp.where((s >= lo) & jnp.logical_not(sure), 1.0, 0.0)
        quota = kf - c_hi
        tri = jnp.where(_iota((LANES, LANES), 0) <= _iota((LANES, LANES), 1), 1.0, 0.0).astype(BF16)
        off = jnp.zeros((rows, 1), F32)
        pieces = []
        for blk in range(keys // LANES):
            bb = band[:, blk * LANES:(blk + 1) * LANES]
            rank = _dot(bb.astype(BF16), tri) + off
            pieces.append(jnp.where((bb > 0.5) & (rank <= quota), 1.0, 0.0))
            off = off + jnp.sum(bb, axis=-1, keepdims=True)
        keep = jnp.concatenate(pieces, axis=-1)
        return jnp.where(sure, 1.0, keep)

    sel = lax.cond(any_tied, ranked, plain)
    return jnp.where(small, adm_f, sel)


def _dsa_prompt_body(qi, q_ref, k_ref, v_ref, iq_ref, ik_ref, iw_ref, o_ref, *, ext, k_top, n_heads):
    tq = q_ref.shape[0]
    q_per_kv = n_heads // ATT_KV_HEADS
    ik = ik_ref[0:ext, :]
    iw = iw_ref[...]
    score = jnp.zeros((tq, ext), F32)
    for h in range(IDX_HEADS):
        s = _dot_nt(iq_ref[:, h * IDX_DIM:(h + 1) * IDX_DIM], ik)
        score = score + jnp.maximum(s * (IDX_DIM ** -0.5), 0.0) * iw[:, h:h + 1]

    q_pos = qi * tq + _iota((tq, ext), 0)
    adm = _iota((tq, ext), 1) <= q_pos
    sel = _select_topk(score, adm, k_top) > 0.5

    for g in range(ATT_KV_HEADS):
        kg = k_ref[0:ext, g * ATT_HEAD_DIM:(g + 1) * ATT_HEAD_DIM]
        vg = v_ref[0:ext, g * ATT_HEAD_DIM:(g + 1) * ATT_HEAD_DIM]
        heads = [g * q_per_kv + hh for hh in range(q_per_kv)]
        q_stack = jnp.concatenate([q_ref[:, h * ATT_HEAD_DIM:(h + 1) * ATT_HEAD_DIM] for h in heads], axis=0)
        logits = _dot_nt(q_stack, kg) * (ATT_HEAD_DIM ** -0.5)
        ps, dens = [], []
        for hh in range(q_per_kv):
            lg = jnp.where(sel, logits[hh * tq:(hh + 1) * tq], -jnp.inf)
            p = jnp.exp(lg - jnp.max(lg, axis=-1, keepdims=True))
            dens.append(jnp.sum(p, axis=-1, keepdims=True))
            ps.append(p.astype(BF16))
        o_stack = _dot(jnp.concatenate(ps, axis=0), vg)
        for hh, h in enumerate(heads):
            o = o_stack[hh * tq:(hh + 1) * tq] / dens[hh]
            o_ref[:, h * ATT_HEAD_DIM:(h + 1) * ATT_HEAD_DIM] = o.astype(o_ref.dtype)


def _dsa_prompt_kernel(q_ref, k_ref, v_ref, iq_ref, ik_ref, iw_ref, o_ref, *, k_top, n_heads, unit):
    qi = pl.program_id(1)
    tq = q_ref.shape[0]
    seq = k_ref.shape[0]
    n_units = ((qi + 1) * tq + unit - 1) // unit
    for c in range(1, seq // unit + 1):
        @pl.when(n_units == c)
        def _(c=c):
            _dsa_prompt_body(qi, q_ref, k_ref, v_ref, iq_ref, ik_ref, iw_ref, o_ref,
                             ext=c * unit, k_top=k_top, n_heads=n_heads)


def _dsa_prompt(q, k, v, iq, ik, iw, batch, seq, d_model):
    tq = 128
    nq = seq // tq
    unit = min(512, seq)
    k_top = min(TOPK_MAX, seq // 4)
    kv_w = ATT_KV_HEADS * ATT_HEAD_DIM
    return pl.pallas_call(
        functools.partial(_dsa_prompt_kernel, k_top=k_top, n_heads=d_model // ATT_HEAD_DIM, unit=unit),
        grid=(batch, nq),
        in_specs=[pl.BlockSpec((tq, d_model), lambda b, i: (b * nq + i, 0)),
                  pl.BlockSpec((seq, kv_w), lambda b, i: (b, 0)),
                  pl.BlockSpec((seq, kv_w), lambda b, i: (b, 0)),
                  pl.BlockSpec((tq, IDX_HEADS * IDX_DIM), lambda b, i: (b * nq + i, 0)),
                  pl.BlockSpec((seq, IDX_DIM), lambda b, i: (b, 0)),
                  pl.BlockSpec((tq, LANES), lambda b, i: (b * nq + i, 0))],
        out_specs=pl.BlockSpec((tq, d_model), lambda b, i: (b * nq + i, 0)),
        out_shape=jax.ShapeDtypeStruct((batch * seq, d_model), BF16),
        compiler_params=_params("parallel", "arbitrary"),
        name="dsa_prompt",
    )(q, k, v, iq, ik, iw)


def _dsa_score_kernel(pt_ref, iq_ref, iw_ref, *rest):
    kpages, o_ref = rest[:-1], rest[-1]
    a = iq_ref[...].astype(BF16)
    w = iw_ref[...]
    for t, kpage_ref in enumerate(kpages):
        s = _dot_nt(a, kpage_ref[...].astype(BF16))
        s = jnp.maximum(s * (IDX_DIM ** -0.5), 0.0) * w
        o_ref[t:t + 1, :] = jnp.sum(s, axis=0, keepdims=True)


def _dsa_scores(page_table, iq, iw, cache_idx_k, layer):
    bd, n_pages = page_table.shape
    page = cache_idx_k.shape[2]
    pg = min(16, n_pages)
    page_specs = [pl.BlockSpec((None, None, page, IDX_DIM),
                               functools.partial(lambda b, s, pt, t: (pt[b, s * pg + t], layer, 0, 0), t=t))
                  for t in range(pg)]
    grid_spec = pltpu.PrefetchScalarGridSpec(
        num_scalar_prefetch=1,
        grid=(bd, n_pages // pg),
        in_specs=[pl.BlockSpec((None, IDX_HEADS, IDX_DIM), lambda b, s, pt: (b, 0, 0)),
                  pl.BlockSpec((None, IDX_HEADS, 1), lambda b, s, pt: (b, 0, 0))] + page_specs,
        out_specs=pl.BlockSpec((None, None, pg, page), lambda b, s, pt: (b, s, 0, 0)),
    )
    out = pl.pallas_call(
        _dsa_score_kernel,
        grid_spec=grid_spec,
        out_shape=jax.ShapeDtypeStruct((bd, n_pages // pg, pg, page), F32),
        compiler_params=_params("parallel", "arbitrary"),
        name="dsa_sample_scores",
    )(page_table, iq.reshape(bd, IDX_HEADS, IDX_DIM), iw[:, :IDX_HEADS].reshape(bd, IDX_HEADS, 1),
      *([cache_idx_k] * pg))
    return out.reshape(bd, n_pages * page)


def _dsa_sample_select_kernel(sc_ref, iq_ref, ik_ref, iw_ref, o_ref, *, k_top):
    bd, past = sc_ref.shape
    iw = iw_ref[...]
    ik = ik_ref[...]
    s_new = jnp.zeros((bd, 1), F32)
    for h in range(IDX_HEADS):
        d = jnp.sum(iq_ref[:, h * IDX_DIM:(h + 1) * IDX_DIM] * ik, axis=-1, keepdims=True)
        s_new = s_new + jnp.maximum(d * (IDX_DIM ** -0.5), 0.0) * iw[:, h:h + 1]
    lane = _iota((bd, LANES), 1)
    tail = jnp.where(lane == 0, s_new, 0.0)
    score = jnp.concatenate([sc_ref[...], tail], axis=-1)
    adm = _iota((bd, past + LANES), 1) <= past
    o_ref[...] = _select_topk(score, adm, k_top)


def _dsa_sample_select(scores, iq, ik, iw):
    bd, past = scores.shape
    k_top = min(TOPK_MAX, (past + 1) // 4)
    return pl.pallas_call(
        functools.partial(_dsa_sample_select_kernel, k_top=k_top),
        out_shape=jax.ShapeDtypeStruct((bd, past + LANES), F32),
        compiler_params=pltpu.CompilerParams(vmem_limit_bytes=VMEM_LIMIT_BYTES),
        name="dsa_sample_select",
    )(scores, iq, ik, iw)


def _dsa_sample_attend_kernel(pt_ref, q_ref, *rest, pg):
    kpages, vpages = rest[:pg], rest[pg:2 * pg]
    sel_ref, seltail_ref, knew_ref, vnew_ref, o_ref, m_ref, l_ref, acc_ref = rest[2 * pg:]
    p = pl.program_id(1)
    n_pages = pl.num_programs(1) - 1
    n_heads = q_ref.shape[0]
    q_per_kv = n_heads // ATT_KV_HEADS
    scale = ATT_HEAD_DIM ** -0.5
    head_group = _iota((n_heads, LANES), 0) // q_per_kv

    @pl.when(p == 0)
    def _():
        m_ref[...] = jnp.full(m_ref.shape, -1e30, F32)
        l_ref[...] = jnp.zeros(l_ref.shape, F32)
        acc_ref[...] = jnp.zeros(acc_ref.shape, F32)

    def update(logits, sel, pv_fn):
        logits = jnp.where(sel, logits, -1e30)
        m_old = m_ref[...]
        m_new = jnp.maximum(m_old, jnp.max(logits, axis=-1, keepdims=True))
        alpha = jnp.exp(m_old - m_new)
        pr = jnp.where(sel, jnp.exp(logits - m_new), 0.0)
        l_ref[...] = alpha * l_ref[...] + jnp.sum(pr, axis=-1, keepdims=True)
        acc_ref[...] = alpha * acc_ref[...] + pv_fn(pr)
        m_ref[...] = m_new

    @pl.when(p < n_pages)
    def _():
        q = q_ref[...]
        rows = kpages[0].shape[0]
        logits = jnp.concatenate([_dot_nt(q, kpages[t][...].astype(BF16)) for t in range(pg)], axis=1) * scale
        spread = jnp.where(_iota((LANES, rows), 1) // ATT_KV_HEADS == _iota((LANES, rows), 0), 1.0, 0.0).astype(BF16)
        sel_rows = _dot(sel_ref[...].astype(BF16), spread)
        sel_keys = jnp.concatenate([sel_rows[t:t + 1, :] for t in range(pg)], axis=1) > 0.5
        own_head = _iota(logits.shape, 1) % ATT_KV_HEADS == _iota(logits.shape, 0) // q_per_kv
        sel = own_head & jnp.broadcast_to(sel_keys, logits.shape)

        def pv(pr):
            out = jnp.zeros((n_heads, ATT_HEAD_DIM), F32)
            for t in range(pg):
                out = out + _dot(pr[:, t * rows:(t + 1) * rows].astype(BF16), vpages[t][...].astype(BF16))
            return out

        update(logits, sel, pv)

    @pl.when(p == n_pages)
    def _():
        qf = q_ref[...].astype(F32)
        kexp = jnp.zeros((n_heads, ATT_HEAD_DIM), F32)
        vexp = jnp.zeros((n_heads, ATT_HEAD_DIM), F32)
        for g in range(ATT_KV_HEADS):
            kexp = jnp.where(head_group == g, knew_ref[:, g * ATT_HEAD_DIM:(g + 1) * ATT_HEAD_DIM], kexp)
            vexp = jnp.where(head_group == g, vnew_ref[:, g * ATT_HEAD_DIM:(g + 1) * ATT_HEAD_DIM], vexp)
        lg = jnp.sum(qf * kexp, axis=-1, keepdims=True) * scale
        sel_new = seltail_ref[:, 0:1] > 0.5
        lane0 = _iota((n_heads, LANES), 1) == 0
        logits = jnp.broadcast_to(lg, (n_heads, LANES))
        sel = lane0 & jnp.broadcast_to(sel_new, (n_heads, LANES))
        update(logits, sel, lambda pr: jnp.sum(pr, axis=-1, keepdims=True) * vexp)
        o_ref[...] = (acc_ref[...] / l_ref[...]).astype(o_ref.dtype)


def _dsa_sample_attend(page_table, q, cache_k, cache_v, sel, k_new, v_new, layer, d_model):
    bd, n_pages = page_table.shape
    page = cache_k.shape[2]
    n_heads = d_model // ATT_HEAD_DIM
    kv_w = ATT_KV_HEADS * ATT_HEAD_DIM
    pg = min(8, n_pages)
    n_steps = n_pages // pg
    last = n_steps - 1
    page_specs = [pl.BlockSpec((None, None, page * ATT_KV_HEADS, ATT_HEAD_DIM),
                               functools.partial(
                                   lambda b, s, pt, t: (pt[b, jnp.minimum(s, last) * pg + t], layer, 0, 0), t=t))
                  for t in range(pg)]
    grid_spec = pltpu.PrefetchScalarGridSpec(
        num_scalar_prefetch=1,
        grid=(bd, n_steps + 1),
        in_specs=[pl.BlockSpec((None, n_heads, ATT_HEAD_DIM), lambda b, s, pt: (b, 0, 0))]
        + page_specs + page_specs
        + [pl.BlockSpec((None, None, pg, page), lambda b, s, pt: (b, jnp.minimum(s, last), 0, 0)),
           pl.BlockSpec((None, 1, page), lambda b, s, pt: (b, 0, 0)),
           pl.BlockSpec((None, 1, kv_w), lambda b, s, pt: (b, 0, 0)),
           pl.BlockSpec((None, 1, kv_w), lambda b, s, pt: (b, 0, 0))],
        out_specs=pl.BlockSpec((None, n_heads, ATT_HEAD_DIM), lambda b, s, pt: (b, 0, 0)),
        scratch_shapes=[pltpu.VMEM((n_heads, 1), F32), pltpu.VMEM((n_heads, 1), F32),
                        pltpu.VMEM((n_heads, ATT_HEAD_DIM), F32)],
    )
    past = n_pages * page
    n_pool, depth = cache_k.shape[:2]
    ck = cache_k.reshape(n_pool, depth, page * ATT_KV_HEADS, ATT_HEAD_DIM)
    cv = cache_v.reshape(n_pool, depth, page * ATT_KV_HEADS, ATT_HEAD_DIM)
    out = pl.pallas_call(
        functools.partial(_dsa_sample_attend_kernel, pg=pg),
        grid_spec=grid_spec,
        out_shape=jax.ShapeDtypeStruct((bd, n_heads, ATT_HEAD_DIM), BF16),
        compiler_params=_params("parallel", "arbitrary"),
        name="dsa_sample_attend",
    )(page_table, q.reshape(bd, n_heads, ATT_HEAD_DIM), *([ck] * pg), *([cv] * pg),
      sel[:, :past].reshape(bd, n_steps, pg, page), sel[:, past:].reshape(bd, 1, page),
      k_new.reshape(bd, 1, kv_w), v_new.reshape(bd, 1, kv_w))
    return out.reshape(bd, d_model)


def _ssd_kernel(z_ref, xs_ref, bc_ref, dt_ref, cw_ref, cb_ref, alog_ref, dtb_ref, dfull_ref, ng_ref,
                e_ref, et_ref, conv0_ref, s0_ref, y_ref, convn_ref, sn_ref, ext_ref, st_ref, yb_ref):
    c = pl.program_id(1)
    nc = pl.num_programs(1)
    q = xs_ref.shape[0]
    inner = xs_ref.shape[1]
    gw = inner // SSM_GROUPS
    heads_pg = gw // SSM_HEAD_DIM
    n_heads = inner // SSM_HEAD_DIM
    pad = 8
    tail = CONV_WIDTH - 1

    @pl.when(c == 0)
    def _():
        ext_ref[0:pad, :] = conv0_ref[...]
        st_ref[...] = s0_ref[...]

    ext_ref[pad:pad + q, 0:inner] = xs_ref[...]
    ext_ref[pad:pad + q, inner:] = bc_ref[...]
    conv = cb_ref[...] + cw_ref[0:1, :] * ext_ref[pl.ds(pad - 3, q), :]
    for i in range(1, CONV_WIDTH):
        conv = conv + cw_ref[i:i + 1, :] * ext_ref[pl.ds(pad - 3 + i, q), :]

    @pl.when(c == nc - 1)
    def _():
        convn_ref[...] = ext_ref[pl.ds(pad + q - tail, tail), :]

    ext_ref[0:pad, :] = ext_ref[q:q + pad, :]
    act = _silu(conv)
    xs = act[:, :inner]
    bm = act[:, inner:inner + SSM_GROUPS * SSM_STATE].astype(BF16)
    cm = act[:, inner + SSM_GROUPS * SSM_STATE:].astype(BF16)

    dt = _softplus(dt_ref[...] + dtb_ref[...])
    la = dt * (-jnp.exp(alog_ref[...]))
    tril = jnp.where(_iota((q, q), 1) <= _iota((q, q), 0), 1.0, 0.0).astype(BF16)
    acs = _sel_left(tril, la)
    acs_t = acs.T
    e = e_ref[...]
    acs_full = _sel_right(acs, e)
    dt_full = _sel_right(dt, e)
    exp_acs = jnp.exp(acs_full)
    to_end = jnp.exp(acs_full[q - 1:q, :] - acs_full)
    v = xs * dt_full
    vb = v.astype(BF16)
    vte = (v * to_end).astype(BF16)
    causal = _iota((q, q), 1) <= _iota((q, q), 0)
    lane_lo = _iota((q, LANES), 1) < SSM_HEAD_DIM
    zero_b = jnp.zeros((q, LANES), BF16)

    a_last = jnp.broadcast_to(acs_t[:, q - 1:q], (LANES, LANES))
    decay_rows = jnp.exp(_sel_left(et_ref[...], a_last))

    for g in range(SSM_GROUPS):
        bg = bm[:, g * SSM_STATE:(g + 1) * SSM_STATE]
        cg = cm[:, g * SSM_STATE:(g + 1) * SSM_STATE]
        qk = _dot_nt(cg, bg)
        rows = slice(g * gw, (g + 1) * gw)
        s_old = st_ref[rows, :]
        y_off = _dot_nt(cg, s_old.astype(BF16))
        new_state = _dot_tn(vte[:, rows], bg)
        st_ref[rows, :] = s_old * decay_rows[rows, :] + new_state
        yb_ref[:, rows] = y_off * exp_acs[:, rows]
        for pr in range(heads_pg // 2):
            h0 = g * heads_pg + 2 * pr
            ms = []
            for h in (h0, h0 + 1):
                seg = acs[:, h:h + 1] - acs_t[h:h + 1, :]
                ms.append((qk * jnp.exp(jnp.where(causal, seg, -jnp.inf))).astype(BF16))
            m2 = jnp.concatenate(ms, axis=1)
            cols = slice(h0 * SSM_HEAD_DIM, (h0 + 2) * SSM_HEAD_DIM)
            vp = vb[:, cols]
            v2 = jnp.concatenate([jnp.where(lane_lo, vp, zero_b), jnp.where(lane_lo, zero_b, vp)], axis=0)
            yb_ref[:, cols] += _dot(m2, v2)

    y = (yb_ref[...] + xs * dfull_ref[...]) * _silu(z_ref[...])
    for g in range(SSM_GROUPS):
        yg = y[:, g * gw:(g + 1) * gw]
        yn = yg * lax.rsqrt(jnp.mean(yg * yg, axis=-1, keepdims=True) + NORM_EPS)
        y_ref[:, g * gw:(g + 1) * gw] = (yn * ng_ref[:, g * gw:(g + 1) * gw]).astype(y_ref.dtype)

    @pl.when(c == nc - 1)
    def _():
        sn_ref[...] = st_ref[...]


def _ssd_prompt(zx, dt, wts, conv0p, s0, batch, seq, inner):
    q = SCAN_CHUNK
    nc = seq // q
    conv_ch = inner + 2 * SSM_GROUPS * SSM_STATE
    bc_w = conv_ch - inner
    full = lambda shape: pl.BlockSpec(shape, lambda b, c: (0,) * len(shape))
    return pl.pallas_call(
        _ssd_kernel,
        grid=(batch, nc),
        in_specs=[pl.BlockSpec((q, inner), lambda b, c: (b * nc + c, 0)),
                  pl.BlockSpec((q, inner), lambda b, c: (b * nc + c, 1)),
                  pl.BlockSpec((q, bc_w), lambda b, c: (b * nc + c, 2 * inner // bc_w)),
                  pl.BlockSpec((q, LANES), lambda b, c: (b * nc + c, 0)),
                  full((CONV_WIDTH, conv_ch)), full((1, conv_ch)), full((1, LANES)), full((1, LANES)),
                  full((1, inner)), full((1, inner)), full((LANES, inner)), full((inner, LANES)),
                  pl.BlockSpec((None, 8, conv_ch), lambda b, c: (b, 0, 0)),
                  pl.BlockSpec((None, inner, SSM_STATE), lambda b, c: (b, 0, 0))],
        out_specs=[pl.BlockSpec((q, inner), lambda b, c: (b * nc + c, 0)),
                   pl.BlockSpec((None, CONV_WIDTH - 1, conv_ch), lambda b, c: (b, 0, 0)),
                   pl.BlockSpec((None, inner, SSM_STATE), lambda b, c: (b, 0, 0))],
        out_shape=[jax.ShapeDtypeStruct((batch * seq, inner), BF16),
                   jax.ShapeDtypeStruct((batch, CONV_WIDTH - 1, conv_ch), F32),
                   jax.ShapeDtypeStruct((batch, inner, SSM_STATE), F32)],
        scratch_shapes=[pltpu.VMEM((q + 8, conv_ch), F32), pltpu.VMEM((inner, SSM_STATE), F32),
                        pltpu.VMEM((q, inner), F32)],
        compiler_params=_params("parallel", "arbitrary"),
        name="ssd_prompt",
    )(zx, zx, zx, dt, wts["conv_w"], wts["conv_b"], wts["a_log"], wts["dt_bias"], wts["d_full"],
      wts["ssm_norm"], wts["e"], wts["et"], conv0p, s0)


def _ret_kernel(rq_ref, rk_ref, rv_ref, rg_ref, cos_ref, sin_ref, dmat_ref, rin_ref, tend_ref, cd_ref,
                s0_ref, y_ref, sn_ref, st_ref):
    c = pl.program_id(1)
    nc = pl.num_programs(1)
    kd = st_ref.shape[2]
    vd = st_ref.shape[1]

    @pl.when(c == 0)
    def _():
        st_ref[...] = s0_ref[...]

    cos, sin = cos_ref[...], sin_ref[...]

    def rot(x):
        return x * cos + pltpu.roll(x, kd // 2, 1) * sin

    for h in range(RET_HEADS):
        qr = rot(rq_ref[:, h * kd:(h + 1) * kd]).astype(BF16)
        kr = (rot(rk_ref[:, h * kd:(h + 1) * kd]) * (kd ** -0.5)).astype(BF16)
        vh = rv_ref[:, h * vd:(h + 1) * vd]
        m = (_dot_nt(qr, kr) * dmat_ref[h]).astype(BF16)
        s_old = st_ref[h]
        y = _dot(m, vh.astype(BF16)) + _dot_nt(qr, s_old.astype(BF16)) * rin_ref[h]
        st_ref[h] = s_old * cd_ref[h] + _dot_tn((vh * tend_ref[h]).astype(BF16), kr)
        yn = y * lax.rsqrt(jnp.mean(y * y, axis=-1, keepdims=True) + NORM_EPS)
        y_ref[:, h * vd:(h + 1) * vd] = (_silu(rg_ref[:, h * vd:(h + 1) * vd]) * yn).astype(y_ref.dtype)

    @pl.when(c == nc - 1)
    def _():
        sn_ref[...] = st_ref[...]


def _ret_prompt(ret, tabs, consts, s0, batch, seq, d_model):
    q = SCAN_CHUNK
    nc = seq // q
    vd = d_model // RET_HEADS
    kd = vd // 2
    qk_w = RET_HEADS * kd
    full = lambda shape: pl.BlockSpec(shape, lambda b, c: (0,) * len(shape))
    return pl.pallas_call(
        _ret_kernel,
        grid=(batch, nc),
        in_specs=[pl.BlockSpec((q, qk_w), lambda b, c: (b * nc + c, 0)),
                  pl.BlockSpec((q, qk_w), lambda b, c: (b * nc + c, 1)),
                  pl.BlockSpec((q, d_model), lambda b, c: (b * nc + c, 1)),
                  pl.BlockSpec((q, d_model), lambda b, c: (b * nc + c, 2)),
                  pl.BlockSpec((q, kd), lambda b, c: (c, 0)),
                  pl.BlockSpec((q, kd), lambda b, c: (c, 0)),
                  full((RET_HEADS, q, q)), full((RET_HEADS, q, 1)), full((RET_HEADS, q, 1)),
                  full((RET_HEADS, 1, kd)),
                  pl.BlockSpec((None, RET_HEADS, vd, kd), lambda b, c: (b, 0, 0, 0))],
        out_specs=[pl.BlockSpec((q, d_model), lambda b, c: (b * nc + c, 0)),
                   pl.BlockSpec((None, RET_HEADS, vd, kd), lambda b, c: (b, 0, 0, 0))],
        out_shape=[jax.ShapeDtypeStruct((batch * seq, d_model), BF16),
                   jax.ShapeDtypeStruct((batch, RET_HEADS, vd, kd), F32)],
        scratch_shapes=[pltpu.VMEM((RET_HEADS, vd, kd), F32)],
        compiler_params=_params("parallel", "arbitrary"),
        name="ret_prompt",
    )(ret, ret, ret, ret, tabs[0], tabs[1], consts["dmat"], consts["rin"], consts["tend"], consts["cd"], s0)


def _ssd_pre_kernel(zx_ref, dt_ref, cbuf_ref, cw_ref, cb_ref, alog_ref, dtb_ref, e_ref,
                    convn_ref, v_ref, da_ref, xs_ref, bm_ref, cm_ref, *, inner):
    x = zx_ref[:, inner:]
    conv = cb_ref[...] + cw_ref[CONV_WIDTH - 1:CONV_WIDTH, :] * x
    for i in range(CONV_WIDTH - 1):
        conv = conv + cw_ref[i:i + 1, :] * cbuf_ref[i]
    for i in range(CONV_WIDTH - 2):
        convn_ref[i] = cbuf_ref[i + 1]
    convn_ref[CONV_WIDTH - 2] = x
    act = _silu(conv)
    xs = act[:, :inner]
    n_bc = SSM_GROUPS * SSM_STATE
    dt = _softplus(dt_ref[...] + dtb_ref[...])
    la = dt * (-jnp.exp(alog_ref[...]))
    e = e_ref[...]
    v_ref[...] = xs * _sel_right(dt, e)
    da_ref[...] = jnp.exp(_sel_right(la, e))
    xs_ref[...] = xs
    bm_ref[...] = act[:, inner:inner + n_bc]
    cm_ref[...] = act[:, inner + n_bc:]


def _ssd_pre(zx, dt, cbuf, wts, inner):
    bd = zx.shape[0]
    conv_ch = inner + 2 * SSM_GROUPS * SSM_STATE
    n_bc = SSM_GROUPS * SSM_STATE
    sd = jax.ShapeDtypeStruct
    return pl.pallas_call(
        functools.partial(_ssd_pre_kernel, inner=inner),
        out_shape=[sd((CONV_WIDTH - 1, bd, conv_ch), F32), sd((bd, inner), F32), sd((bd, inner), F32),
                   sd((bd, inner), F32), sd((bd, n_bc), F32), sd((bd, n_bc), F32)],
        compiler_params=pltpu.CompilerParams(vmem_limit_bytes=VMEM_LIMIT_BYTES),
        name="ssd_step_pre",
    )(zx, dt, cbuf, wts["conv_w"], wts["conv_b"], wts["a_log"], wts["dt_bias"], wts["e"])


def _state_step_kernel(v_ref, d_ref, k_ref, q_ref, s_ref, y_ref, sn_ref, vt_ref, dt_ref, pad_ref):
    b = pl.program_id(0)
    bd, rows = v_ref.shape
    n_groups = k_ref.shape[0]
    gr = rows // n_groups

    @pl.when(b == 0)
    def _():
        pad_ref[...] = jnp.zeros(pad_ref.shape, F32)
        for src, dst in ((v_ref, vt_ref), (d_ref, dt_ref)):
            pad_ref[0:bd, :] = src[...]
            for t in range(rows // LANES):
                dst[t * LANES:(t + 1) * LANES, :] = pad_ref[:, t * LANES:(t + 1) * LANES].T

    onehot = jnp.where(_iota((LANES, LANES), 0) == b, 1.0, 0.0).astype(BF16)
    v_rows = _sel_right(vt_ref[...], onehot)
    d_rows = _sel_right(dt_ref[...], onehot)
    for g in range(n_groups):
        rs = slice(g * gr, (g + 1) * gr)
        s_new = s_ref[rs, :] * d_rows[rs, :] + v_rows[rs, :] * k_ref[g:g + 1, :]
        sn_ref[rs, :] = s_new
        q8 = jnp.broadcast_to(q_ref[g:g + 1, :], (8, q_ref.shape[1])).astype(BF16)
        y_ref[pl.ds(b, 1), rs] = _dot_nt(q8, s_new.astype(BF16))[0:1, :]


def _state_step(v, d, k, q, state):
    bd, rows = v.shape
    n_groups, kd = k.shape[1], k.shape[2]
    return pl.pallas_call(
        _state_step_kernel,
        grid=(bd,),
        in_specs=[pl.BlockSpec((bd, rows), lambda b: (0, 0)),
                  pl.BlockSpec((bd, rows), lambda b: (0, 0)),
                  pl.BlockSpec((None, n_groups, kd), lambda b: (b, 0, 0)),
                  pl.BlockSpec((None, n_groups, kd), lambda b: (b, 0, 0)),
                  pl.BlockSpec((None, rows, kd), lambda b: (b, 0, 0))],
        out_specs=[pl.BlockSpec((bd, rows), lambda b: (0, 0)),
                   pl.BlockSpec((None, rows, kd), lambda b: (b, 0, 0))],
        out_shape=[jax.ShapeDtypeStruct((bd, rows), F32),
                   jax.ShapeDtypeStruct((bd, rows, kd), F32)],
        scratch_shapes=[pltpu.VMEM((rows, LANES), F32), pltpu.VMEM((rows, LANES), F32),
                        pltpu.VMEM((LANES, rows), F32)],
        compiler_params=_params("arbitrary"),
        name="state_step",
    )(v, d, k, q, state)


def _ssd_post_kernel(y_ref, xs_ref, zx_ref, dfull_ref, ng_ref, o_ref, *, inner):
    y = (y_ref[...] + xs_ref[...] * dfull_ref[...]) * _silu(zx_ref[:, :inner])
    gw = inner // SSM_GROUPS
    for g in range(SSM_GROUPS):
        yg = y[:, g * gw:(g + 1) * gw]
        yn = yg * lax.rsqrt(jnp.mean(yg * yg, axis=-1, keepdims=True) + NORM_EPS)
        o_ref[:, g * gw:(g + 1) * gw] = (yn * ng_ref[:, g * gw:(g + 1) * gw]).astype(o_ref.dtype)


def _ssd_post(y, xs, zx, wts, inner):
    return pl.pallas_call(
        functools.partial(_ssd_post_kernel, inner=inner),
        out_shape=jax.ShapeDtypeStruct(y.shape, BF16),
        compiler_params=pltpu.CompilerParams(vmem_limit_bytes=VMEM_LIMIT_BYTES),
        name="ssd_step_post",
    )(y, xs, zx, wts["d_full"], wts["ssm_norm"])


def _ret_pre_kernel(ret_ref, cos_ref, sin_ref, q_ref, k_ref, *, kd):
    cos, sin = cos_ref[...], sin_ref[...]
    qk_w = RET_HEADS * kd

    def rot(x):
        return x * cos + pltpu.roll(x, kd // 2, 1) * sin

    for h in range(RET_HEADS):
        q_ref[:, h * kd:(h + 1) * kd] = rot(ret_ref[:, h * kd:(h + 1) * kd])
        k_ref[:, h * kd:(h + 1) * kd] = rot(ret_ref[:, qk_w + h * kd:qk_w + (h + 1) * kd]) * (kd ** -0.5)


def _ret_pre(ret, tabs, kd):
    bd = ret.shape[0]
    sd = jax.ShapeDtypeStruct
    return pl.pallas_call(
        functools.partial(_ret_pre_kernel, kd=kd),
        out_shape=[sd((bd, RET_HEADS * kd), F32), sd((bd, RET_HEADS * kd), F32)],
        compiler_params=pltpu.CompilerParams(vmem_limit_bytes=VMEM_LIMIT_BYTES),
        name="ret_step_pre",
    )(ret, tabs[0], tabs[1])


def _ret_post_kernel(y_ref, ret_ref, o_ref, *, vd, g_off):
    for h in range(RET_HEADS):
        y = y_ref[:, h * vd:(h + 1) * vd]
        yn = y * lax.rsqrt(jnp.mean(y * y, axis=-1, keepdims=True) + NORM_EPS)
        g = ret_ref[:, g_off + h * vd:g_off + (h + 1) * vd]
        o_ref[:, h * vd:(h + 1) * vd] = (_silu(g) * yn).astype(o_ref.dtype)


def _ret_post(y, ret, vd, g_off):
    return pl.pallas_call(
        functools.partial(_ret_post_kernel, vd=vd, g_off=g_off),
        out_shape=jax.ShapeDtypeStruct(y.shape, BF16),
        compiler_params=pltpu.CompilerParams(vmem_limit_bytes=VMEM_LIMIT_BYTES),
        name="ret_step_post",
    )(y, ret)


def _rope_tables(pos, rot_dim, width, theta):
    half = rot_dim // 2
    inv_freq = theta ** (-jnp.arange(half, dtype=F32) * (2.0 / rot_dim))
    ang = pos.astype(F32)[:, None] * inv_freq[None, :]
    cos, sin = jnp.cos(ang), jnp.sin(ang)
    n = pos.shape[0]
    ones = jnp.ones((n, width - rot_dim), F32)
    zeros = jnp.zeros((n, width - rot_dim), F32)
    z_half = jnp.zeros((n, half), F32)
    c = jnp.concatenate([cos, cos, ones], axis=1)
    sa = jnp.concatenate([-sin, z_half, zeros], axis=1)
    sb = jnp.concatenate([z_half, sin, zeros], axis=1)
    rep = LANES // width
    return tuple(jnp.tile(t, (1, rep)) for t in (c, sa, sb))


def _ret_tables(pos, kd):
    half = kd // 2
    inv_freq = RET_THETA ** (-jnp.arange(half, dtype=F32) * (2.0 / kd))
    ang = pos.astype(F32)[:, None] * inv_freq[None, :]
    cos, sin = jnp.cos(ang), jnp.sin(ang)
    return jnp.concatenate([cos, cos], axis=1), jnp.concatenate([-sin, sin], axis=1)


def _ret_consts(q):
    lg = jnp.log(1.0 - 2.0 ** (-5.0 - jnp.arange(RET_HEADS, dtype=F32)))
    i = jnp.arange(q, dtype=F32)
    diff = i[:, None] - i[None, :]
    dmat = jnp.where(diff >= 0, jnp.exp(diff[None] * lg[:, None, None]), 0.0)
    rin = jnp.exp((i + 1.0)[None, :] * lg[:, None])[..., None]
    tend = jnp.exp((q - 1.0 - i)[None, :] * lg[:, None])[..., None]
    return lg, {"dmat": dmat, "rin": rin, "tend": tend}


def _pack_layer(l, w_in, conv_w, conv_b, ssm_a_log, ssm_dt_bias, ssm_d, ssm_norm, d_model):
    inner = d_model
    conv_ch = inner + 2 * SSM_GROUPS * SSM_STATE
    n_ssm_heads = inner // SSM_HEAD_DIM
    n_att = d_model
    kv_w = ATT_KV_HEADS * ATT_HEAD_DIM
    iq_w = IDX_HEADS * IDX_DIM
    vd = d_model // RET_HEADS
    kd = vd // 2
    widths = (inner, conv_ch, n_ssm_heads, n_att, kv_w, kv_w, iq_w, IDX_DIM, IDX_HEADS,
              RET_HEADS * kd, RET_HEADS * kd, RET_HEADS * vd, RET_HEADS * vd, N_BRANCH * d_model)
    offs = [0]
    for w in widths:
        offs.append(offs[-1] + w)
    w = w_in[l]
    col = lambda a, b: w[:, offs[a]:offs[b]]
    zpad = lambda n: jnp.zeros((w.shape[0], n), w.dtype)
    pad1 = lambda x: jnp.concatenate([x, jnp.zeros((LANES - x.shape[0],), x.dtype)]).reshape(1, LANES)
    head_of = jnp.arange(inner) // SSM_HEAD_DIM
    e = (jnp.arange(LANES)[:, None] == head_of[None, :]).astype(BF16)
    return {
        "w_ssd": col(0, 2),
        "w_dt": jnp.concatenate([col(2, 3), zpad(LANES - n_ssm_heads)], axis=1),
        "w_att": col(3, 6),
        "w_idx": jnp.concatenate([col(6, 8), zpad(LANES - IDX_DIM), col(8, 9), zpad(LANES - IDX_HEADS)], axis=1),
        "w_ret": col(9, 13),
        "w_gate": col(13, 14),
        "conv_w": conv_w[l], "conv_b": conv_b[l].reshape(1, conv_ch),
        "a_log": pad1(ssm_a_log[l]), "dt_bias": pad1(ssm_dt_bias[l]),
        "d_full": jnp.repeat(ssm_d[l], SSM_HEAD_DIM).reshape(1, inner),
        "ssm_norm": ssm_norm[l].reshape(1, inner),
        "e": e, "et": e.T,
    }


def _mixer_inputs(h, wts, tm):
    zx = _mm(h, wts["w_ssd"], tm, 1024, name="in_proj_ssd")
    dt = _mm(h, wts["w_dt"], tm, LANES, name="in_proj_dt")
    att = _mm(h, wts["w_att"], tm, 1024, name="in_proj_att")
    idx = _mm(h, wts["w_idx"], tm, wts["w_idx"].shape[1], name="in_proj_idx")
    ret = _mm(h, wts["w_ret"], tm, 1024, name="in_proj_ret")
    gl = _mm(h, wts["w_gate"], tm, 1024, name="in_proj_gate")
    return zx, dt, att, idx, ret, gl


def _ffn(grp, x, l, norm_g_l, w_gate, w_up, w_down, which):
    d_ff = w_gate.shape[1]
    h = _normmod(grp, x, norm_g_l, l, which, which + 1)
    act = _gateup(h, w_gate, w_up, grp.tm, 512)
    return _mm_res(grp, act, w_down, x, l, which + 2, MACARON_WEIGHT, grp.tm, 512, d_ff // 4, "ffn_down")


def kernel(x_prompt, x_sample, cache_k, cache_v, cache_idx_k, state_conv, state_ssm, state_ret, page_table,
           c_prompt, c_sample, w_in, conv_w, conv_b, ssm_a_log, ssm_dt_bias, ssm_d, ssm_norm, w_branch, w_out,
           ffn1_gate, ffn1_up, ffn1_down, ffn2_gate, ffn2_up, ffn2_down, norm_g, final_g, ada_w, ada_b):
    bp, seq, d = x_prompt.shape
    bd = x_sample.shape[0]
    inner = d
    conv_ch = inner + 2 * SSM_GROUPS * SSM_STATE
    n_ssm_heads = inner // SSM_HEAD_DIM
    vd = d // RET_HEADS
    kd = vd // 2
    past_len = page_table.shape[1] * cache_k.shape[2]

    mods = _ada(jnp.concatenate([c_sample, c_prompt], axis=0), ada_w, ada_b)
    mods = mods.reshape(DEPTH, bd + bp, N_MOD, d)
    mods_s = jnp.swapaxes(mods[:, :bd], 1, 2)
    mods_p = mods[:, bd:].reshape(DEPTH, bp * N_MOD, 1, d)
    gp = _Group(bp, seq, mods_p, min(1024, seq))
    gs = _Group(bd, 1, mods_s, bd)

    pos_p = jnp.arange(seq)
    pos_s = jnp.full((bd,), past_len)
    rot = ATT_HEAD_DIM // ROPE_FRACTION
    irot = IDX_DIM // ROPE_FRACTION
    tabs_att_p = _rope_tables(pos_p, rot, ATT_HEAD_DIM, ROPE_THETA)
    tabs_idx_p = _rope_tables(pos_p, irot, IDX_DIM, ROPE_THETA)
    tabs_att_s = _rope_tables(pos_s, rot, ATT_HEAD_DIM, ROPE_THETA)
    tabs_idx_s = _rope_tables(pos_s, irot, IDX_DIM, ROPE_THETA)
    tabs_ret_p = _ret_tables(pos_p, kd)
    tabs_ret_s = _ret_tables(pos_s, kd)
    lg, ret_consts = _ret_consts(SCAN_CHUNK)
    ret_consts["cd"] = jnp.broadcast_to(jnp.exp(SCAN_CHUNK * lg)[:, None, None], (RET_HEADS, 1, kd))
    gamma_rows = jnp.broadcast_to(jnp.repeat(jnp.exp(lg), vd)[None, :], (bd, d))

    xp = x_prompt.reshape(bp * seq, d)
    xs = x_sample.reshape(bd, d)
    conv0p = jnp.zeros((bp, 8, conv_ch), F32)
    ssm0 = jnp.zeros((bp, inner, SSM_STATE), F32)
    ret0 = jnp.zeros((bp, RET_HEADS, vd, kd), F32)

    outs_p = {k: [] for k in ("k", "v", "ik", "conv", "ssm", "ret")}
    outs_s = {k: [] for k in ("k", "v", "ik", "conv", "ssm", "ret")}
    tm_p = gp.tm

    for l in range(DEPTH):
        wts = _pack_layer(l, w_in, conv_w, conv_b, ssm_a_log, ssm_dt_bias, ssm_d, ssm_norm, d)

        xp = _ffn(gp, xp, l, norm_g[l, 0], ffn1_gate[l], ffn1_up[l], ffn1_down[l], 0)
        h = _normmod(gp, xp, norm_g[l, 1], l, 3, 4)
        zx, dt, att, idx, ret, gl = _mixer_inputs(h, wts, tm_p)
        y_ssd, conv_new, ssm_new = _ssd_prompt(zx, dt, wts, conv0p, ssm0, bp, seq, inner)
        q_rot, k_rot, v_rows, _, ik_rot, iw, kb, vb, iqb, ikb = _rotary(
            att, idx, tabs_att_p, tabs_idx_p, min(256, seq), d)
        y_att = _dsa_prompt(q_rot, kb, vb, iqb, ikb, iw, bp, seq, d)
        y_ret, ret_new = _ret_prompt(ret, tabs_ret_p, ret_consts, ret0, bp, seq, d)
        merged = _merge([y_ssd, y_att, y_ret], w_branch[l], gl, tm_p, 256)
        xp = _mm_res(gp, merged, w_out[l], xp, l, 5, 1.0, tm_p, 512, d, "out_proj")
        xp = _ffn(gp, xp, l, norm_g[l, 2], ffn2_gate[l], ffn2_up[l], ffn2_down[l], 6)
        outs_p["k"].append(k_rot.reshape(bp, seq, ATT_KV_HEADS, ATT_HEAD_DIM))
        outs_p["v"].append(v_rows.reshape(bp, seq, ATT_KV_HEADS, ATT_HEAD_DIM))
        outs_p["ik"].append(ik_rot.reshape(bp, seq, IDX_DIM))
        outs_p["conv"].append(conv_new)
        outs_p["ssm"].append(ssm_new.reshape(bp, n_ssm_heads, SSM_HEAD_DIM, SSM_STATE))
        outs_p["ret"].append(ret_new)

        xs = _ffn(gs, xs, l, norm_g[l, 0], ffn1_gate[l], ffn1_up[l], ffn1_down[l], 0)
        h = _normmod(gs, xs, norm_g[l, 1], l, 3, 4)
        zx, dt, att, idx, ret, gl = _mixer_inputs(h, wts, bd)
        cbuf = jnp.swapaxes(state_conv[:, l], 0, 1)
        conv_new, v_s, da_s, xs_act, bm, cm = _ssd_pre(zx, dt, cbuf, wts, inner)
        y_raw, ssm_new = _state_step(v_s, da_s, bm.reshape(bd, SSM_GROUPS, SSM_STATE),
                                     cm.reshape(bd, SSM_GROUPS, SSM_STATE),
                                     state_ssm[:, l].reshape(bd, inner, SSM_STATE))
        y_ssd = _ssd_post(y_raw, xs_act, zx, wts, inner)
        q_rot, k_rot, v_rows, iq_rot, ik_rot, iw = _rotary(att, idx, tabs_att_s, tabs_idx_s, bd, d)[:6]
        scores = _dsa_scores(page_table, iq_rot, iw, cache_idx_k, l)
        sel = _dsa_sample_select(scores, iq_rot, ik_rot, iw)
        y_att = _dsa_sample_attend(page_table, q_rot, cache_k, cache_v, sel, k_rot, v_rows, l, d)
        rq, rk = _ret_pre(ret, tabs_ret_s, kd)
        y_raw, ret_new = _state_step(ret[:, 2 * RET_HEADS * kd:2 * RET_HEADS * kd + d], gamma_rows,
                                     rk.reshape(bd, RET_HEADS, kd), rq.reshape(bd, RET_HEADS, kd),
                                     state_ret[:, l].reshape(bd, d, kd))
        y_ret = _ret_post(y_raw, ret, vd, 2 * RET_HEADS * kd + d)
        merged = _merge([y_ssd, y_att, y_ret], w_branch[l], gl, bd, 256)
        xs = _mm_res(gs, merged, w_out[l], xs, l, 5, 1.0, bd, 512, d, "out_proj")
        xs = _ffn(gs, xs, l, norm_g[l, 2], ffn2_gate[l], ffn2_up[l], ffn2_down[l], 6)
        outs_s["k"].append(k_rot.reshape(bd, 1, ATT_KV_HEADS, ATT_HEAD_DIM))
        outs_s["v"].append(v_rows.reshape(bd, 1, ATT_KV_HEADS, ATT_HEAD_DIM))
        outs_s["ik"].append(ik_rot.reshape(bd, 1, IDX_DIM))
        outs_s["conv"].append(jnp.swapaxes(conv_new, 0, 1))
        outs_s["ssm"].append(ssm_new.reshape(bd, n_ssm_heads, SSM_HEAD_DIM, SSM_STATE))
        outs_s["ret"].append(ret_new.reshape(bd, RET_HEADS, vd, kd))

    y_prompt = _final_norm(gp, xp, final_g).reshape(bp, seq, d)
    y_sample = _final_norm(gs, xs, final_g).reshape(bd, 1, d)
    st = lambda a: jnp.stack(a, axis=1)
    order = ("k", "v", "ik", "conv", "ssm", "ret")
    return (y_prompt, y_sample) + tuple(st(outs_p[k]) for k in order) + tuple(st(outs_s[k]) for k in order)
```
